```python
import functools
import jax, jax.numpy as jnp
from jax import lax
import numpy as np

D_MODEL = 4096
BATCH = 4
SEQ = 2048
DEPTH = 2
DEC_BATCH = 128
DEC_SEQ = 1
PAST_LEN = 16384
PAGE_SIZE = 128

W_A = 1024
W_B = 1024
W_C = 1024
N_HEADS = 8
QK_NOPE = 128
QK_ROPE = 64
V_DIM = 128
W_D = N_HEADS * V_DIM
Q_RANK = 1024
KV_RANK = 512
W_MIX = W_A + W_B + W_C + W_D
N_IN = 2 * W_A + 3 * W_B + W_C + Q_RANK + KV_RANK + QK_ROPE
CONV_A_WIDTH = 31
CONV_B_WIDTH = 3
POOL_WINDOWS = (2, 4, 8, 16)
POOL_GROUP = W_C // 4
POOL_BUF = 15
D_FF = 11008
N_EXPERTS = 8
TOP_K = 2
D_FF_EXPERT = 4096
PLE_DIM = 256
ROPE_BASE = 10000.0
Q_BLOCK = 128
EPS = 1e-6
N_DENSE = (DEPTH + 1) // 2
N_MOE = DEPTH // 2

kernel_name = 'hybrid_parallel_groups_conv_pool_mla_step'


def rmsnorm(x, g):
    xf = x.astype(jnp.float32)
    y = xf * lax.rsqrt(jnp.mean(xf * xf, axis=-1, keepdims=True) + EPS)
    return y.astype(x.dtype) * g


def layernorm(x, g, b):
    xf = x.astype(jnp.float32)
    mu = jnp.mean(xf, axis=-1, keepdims=True)
    var = jnp.mean(jnp.square(xf - mu), axis=-1, keepdims=True)
    return ((xf - mu) * lax.rsqrt(var + EPS)).astype(x.dtype) * g + b


def rope_angles(pos):
    inv = ROPE_BASE ** (-jnp.arange(0, QK_ROPE, 2, dtype=jnp.float32) / QK_ROPE)
    ang = pos.astype(jnp.float32)[:, None] * inv[None, :]
    return jnp.cos(ang), jnp.sin(ang)


def apply_rope(x, cos, sin):
    half = QK_ROPE // 2
    x1 = x[..., :half].astype(jnp.float32)
    x2 = x[..., half:].astype(jnp.float32)
    return jnp.concatenate([x1 * cos - x2 * sin, x1 * sin + x2 * cos], axis=-1).astype(x.dtype)


def causal_dwconv(buf, u, w):
    ext = jnp.concatenate([buf, u], axis=1)
    y = lax.conv_general_dilated(ext, w[:, None, :], window_strides=(1,), padding='VALID',
                                 dimension_numbers=('NWC', 'WIO', 'NWC'),
                                 feature_group_count=u.shape[-1])
    return y, ext[:, u.shape[1]:]


def multiscale_pool(buf, v, start_pos, w_pool, scale):
    nb, L = buf.shape[1], v.shape[1]
    ext = jnp.concatenate([buf, v], axis=1)
    cs = jnp.cumsum(ext.astype(jnp.float32), axis=1)
    cs = jnp.concatenate([jnp.zeros_like(cs[:, :1]), cs], axis=1)
    hi = cs[:, nb + 1: nb + 1 + L]
    pos = (start_pos + jnp.arange(L)).astype(jnp.float32)
    outs = []
    for g, win in enumerate(POOL_WINDOWS):
        sl = slice(g * POOL_GROUP, (g + 1) * POOL_GROUP)
        lo = cs[:, nb + 1 - win: nb + 1 - win + L, sl]
        cnt = jnp.minimum(pos + 1.0, float(win))[None, :, None]
        d = (hi[..., sl] - lo) / cnt - v[..., sl].astype(jnp.float32)
        outs.append(jnp.einsum('bsc,cd->bsd', d.astype(v.dtype), w_pool[g]))
    return jnp.concatenate(outs, axis=-1) * scale, ext[:, L:]


def mla_prompt_attend(q_nope, q_pe, latent, k_pe, w_uk, w_uv):
    B, S = q_nope.shape[:2]
    k_nope = jnp.einsum('bsc,chd->bshd', latent, w_uk)
    v = jnp.einsum('bsc,chd->bshd', latent, w_uv)
    k_pos = jnp.arange(S)
    scale = (QK_NOPE + QK_ROPE) ** -0.5

    def block(i):
        start = i * Q_BLOCK
        qn = lax.dynamic_slice_in_dim(q_nope, start, Q_BLOCK, axis=1)
        qp = lax.dynamic_slice_in_dim(q_pe, start, Q_BLOCK, axis=1)
        s = jnp.einsum('bqhd,bkhd->bhqk', qn, k_nope) + jnp.einsum('bqhd,bkd->bhqk', qp, k_pe)
        mask = k_pos[None, :] <= (start + jnp.arange(Q_BLOCK))[:, None]
        p = jax.nn.softmax(jnp.where(mask, s.astype(jnp.float32) * scale, -jnp.inf), axis=-1)
        return jnp.einsum('bhqk,bkhd->bqhd', p.astype(v.dtype), v)

    o = lax.map(block, jnp.arange(S // Q_BLOCK))
    return jnp.moveaxis(o, 0, 1).reshape(B, S, W_D)


def mla_sample_attend(q_nope, q_pe, latent, k_pe, w_uk, w_uv, cache_latent, cache_krope, layer, page_table):
    B, T = q_nope.shape[:2]
    q_lat = jnp.einsum('bthd,chd->bthc', q_nope, w_uk)
    n_past = page_table.shape[1] * PAGE_SIZE
    k_idx = jnp.arange(n_past + T)
    mask = k_idx[None, :] <= (n_past + jnp.arange(T))[:, None]
    scale = (QK_NOPE + QK_ROPE) ** -0.5

    def one(args):
        ql, qp, pages, lat_new, kpe_new = args
        lat = jnp.concatenate([cache_latent[layer, pages].reshape(n_past, KV_RANK), lat_new], axis=0)
        kpe = jnp.concatenate([cache_krope[layer, pages].reshape(n_past, QK_ROPE), kpe_new], axis=0)
        s = jnp.einsum('thc,kc->htk', ql, lat) + jnp.einsum('thd,kd->htk', qp, kpe)
        p = jax.nn.softmax(jnp.where(mask[None], s.astype(jnp.float32) * scale, -jnp.inf), axis=-1)
        return jnp.einsum('htk,kc->thc', p.astype(lat.dtype), lat)

    o_lat = lax.map(one, (q_lat, q_pe, page_table, latent, k_pe))
    return jnp.einsum('bthc,chd->bthd', o_lat, w_uv).reshape(B, T, W_D)


def swiglu(x, w_gate, w_up, w_down):
    h = jax.nn.silu(jnp.einsum('bsd,df->bsf', x, w_gate)) * jnp.einsum('bsd,df->bsf', x, w_up)
    return jnp.einsum('bsf,fd->bsd', h, w_down)


def moe_swiglu(x, w_router, w_gate, w_up, w_down):
    probs = jax.nn.softmax(jnp.einsum('bsd,de->bse', x, w_router).astype(jnp.float32), axis=-1)
    top_p, top_i = lax.top_k(probs, TOP_K)
    top_p = top_p / jnp.sum(top_p, axis=-1, keepdims=True)
    gates = jnp.sum(jax.nn.one_hot(top_i, N_EXPERTS, dtype=jnp.float32) * top_p[..., None], axis=-2).astype(x.dtype)
    out = jnp.zeros_like(x)
    for e in range(N_EXPERTS):
        out = out + gates[..., e:e + 1] * swiglu(x, w_gate[e], w_up[e], w_down[e])
    return out


def hybrid_layer(x, p_l, buf_a, buf_b, buf_c, positions, start_pos, attend, ffn, lw):
    xn = rmsnorm(x, lw['g_pre_mix'])
    u = jnp.einsum('bsd,dn->bsn', xn, lw['w_in'])
    sizes = [W_A, W_A, W_B, W_B, W_B, W_C, Q_RANK, KV_RANK, QK_ROPE]
    a_val, a_gate, b_b, b_c, b_x, c_v, c_q, c_kv, k_pe_raw = jnp.split(u, list(np.cumsum(sizes)[:-1]), axis=-1)
    h_a = a_val * jax.nn.sigmoid(a_gate)
    y_a, nbuf_a = causal_dwconv(buf_a, h_a, lw['conv_a_w'])
    y_a = jax.nn.silu(layernorm(y_a + lw['conv_a_b'], lw['ln_a_g'], lw['ln_a_b']))
    y_b, nbuf_b = causal_dwconv(buf_b, b_c * b_x, lw['conv_b_w'])
    y_b = b_b * y_b
    y_c, nbuf_c = multiscale_pool(buf_c, c_v, start_pos, lw['pool_w'], lw['pool_scale'])
    cos, sin = rope_angles(positions)
    q = jnp.einsum('bsr,rhd->bshd', rmsnorm(c_q, lw['g_q']), lw['w_uq'])
    q_nope = q[..., :QK_NOPE]
    q_pe = apply_rope(q[..., QK_NOPE:], cos[:, None, :], sin[:, None, :])
    latent = rmsnorm(c_kv, lw['g_kv'])
    k_pe = apply_rope(k_pe_raw, cos, sin)
    y_d = attend(q_nope, q_pe, latent, k_pe, lw['w_uk'], lw['w_uv'])
    mix = jnp.einsum('bsm,md->bsd', jnp.concatenate([y_a, y_b, y_c, y_d], axis=-1), lw['w_out'])
    x = x + rmsnorm(mix, lw['g_post_mix'])
    x = x + rmsnorm(ffn(rmsnorm(x, lw['g_pre_ffn'])), lw['g_post_ffn'])
    gate = jax.nn.sigmoid(jnp.einsum('bsd,de->bse', rmsnorm(x, lw['g_ple']), lw['w_ple_gate']))
    x = x + gate * jnp.einsum('bsp,pd->bsd', p_l, lw['w_ple'])
    return x, nbuf_a, nbuf_b, nbuf_c, latent, k_pe


def setup_inputs(seed: int = 0) -> dict:
    key = jax.random.key(seed)
    ks = iter(jax.random.split(key, 64))
    f32 = jnp.float32

    def nrm(shape, scale=1.0):
        return jax.random.normal(next(ks), shape, f32) * scale

    def gain(shape):
        return 1.0 + nrm(shape, 0.05)

    n_pages = PAST_LEN // PAGE_SIZE
    n_used = DEC_BATCH * n_pages
    n_pool = n_used + max(1, n_used // 4)
    page_table = jax.random.permutation(next(ks), n_pool)[:n_used].reshape(DEC_BATCH, n_pages).astype(jnp.int32)
    return {
        'x_prompt': nrm((BATCH, SEQ, D_MODEL)),
        'x_sample': nrm((DEC_BATCH, DEC_SEQ, D_MODEL)),
        'p_prompt': nrm((DEPTH, BATCH, SEQ, PLE_DIM)),
        'p_sample': nrm((DEPTH, DEC_BATCH, DEC_SEQ, PLE_DIM)),
        'state_conv_a': nrm((DEPTH, DEC_BATCH, CONV_A_WIDTH - 1, W_A), 0.5),
        'state_conv_b': nrm((DEPTH, DEC_BATCH, CONV_B_WIDTH - 1, W_B), 0.5),
        'state_pool': nrm((DEPTH, DEC_BATCH, POOL_BUF, W_C)),
        'cache_latent': nrm((DEPTH, n_pool, PAGE_SIZE, KV_RANK)),
        'cache_krope': nrm((DEPTH, n_pool, PAGE_SIZE, QK_ROPE)),
        'page_table': page_table,
        'g_pre_mix': gain((DEPTH, D_MODEL)),
        'w_in': nrm((DEPTH, D_MODEL, N_IN), D_MODEL ** -0.5),
        'conv_a_w': nrm((DEPTH, CONV_A_WIDTH, W_A), CONV_A_WIDTH ** -0.5),
        'conv_a_b': nrm((DEPTH, W_A), 0.02),
        'ln_a_g': gain((DEPTH, W_A)),
        'ln_a_b': nrm((DEPTH, W_A), 0.02),
        'conv_b_w': nrm((DEPTH, CONV_B_WIDTH, W_B), CONV_B_WIDTH ** -0.5),
        'pool_w': nrm((DEPTH, len(POOL_WINDOWS), POOL_GROUP, POOL_GROUP), POOL_GROUP ** -0.5),
        'pool_scale': 1.0 + nrm((DEPTH, W_C), 0.1),
        'g_q': gain((DEPTH, Q_RANK)),
        'w_uq': nrm((DEPTH, Q_RANK, N_HEADS, QK_NOPE + QK_ROPE), Q_RANK ** -0.5),
        'g_kv': gain((DEPTH, KV_RANK)),
        'w_uk': nrm((DEPTH, KV_RANK, N_HEADS, QK_NOPE), KV_RANK ** -0.5),
        'w_uv': nrm((DEPTH, KV_RANK, N_HEADS, V_DIM), KV_RANK ** -0.5),
        'w_out': nrm((DEPTH, W_MIX, D_MODEL), W_MIX ** -0.5),
        'g_post_mix': gain((DEPTH, D_MODEL)),
        'g_pre_ffn': gain((DEPTH, D_MODEL)),
        'g_post_ffn': gain((DEPTH, D_MODEL)),
        'w_ffn_gate': nrm((N_DENSE, D_MODEL, D_FF), D_MODEL ** -0.5),
        'w_ffn_up': nrm((N_DENSE, D_MODEL, D_FF), D_MODEL ** -0.5),
        'w_ffn_down': nrm((N_DENSE, D_FF, D_MODEL), D_FF ** -0.5),
        'w_router': nrm((N_MOE, D_MODEL, N_EXPERTS), D_MODEL ** -0.5),
        'w_exp_gate': nrm((N_MOE, N_EXPERTS, D_MODEL, D_FF_EXPERT), D_MODEL ** -0.5),
        'w_exp_up': nrm((N_MOE, N_EXPERTS, D_MODEL, D_FF_EXPERT), D_MODEL ** -0.5),
        'w_exp_down': nrm((N_MOE, N_EXPERTS, D_FF_EXPERT, D_MODEL), D_FF_EXPERT ** -0.5),
        'g_ple': gain((DEPTH, D_MODEL)),
        'w_ple_gate': nrm((DEPTH, D_MODEL, D_MODEL), D_MODEL ** -0.5),
        'w_ple': nrm((DEPTH, PLE_DIM, D_MODEL), PLE_DIM ** -0.5),
    }


def reference(x_prompt, x_sample, p_prompt, p_sample, state_conv_a, state_conv_b, state_pool,
              cache_latent, cache_krope, page_table, g_pre_mix, w_in, conv_a_w, conv_a_b, ln_a_g, ln_a_b,
              conv_b_w, pool_w, pool_scale, g_q, w_uq, g_kv, w_uk, w_uv, w_out, g_post_mix, g_pre_ffn,
              g_post_ffn, w_ffn_gate, w_ffn_up, w_ffn_down, w_router, w_exp_gate, w_exp_up, w_exp_down,
              g_ple, w_ple_gate, w_ple):
    B, S = x_prompt.shape[:2]
    DB, T = x_sample.shape[:2]
    n_past = page_table.shape[1] * PAGE_SIZE
    pos_prompt = jnp.arange(S)
    pos_sample = n_past + jnp.arange(T)
    dt = x_prompt.dtype
    xp, xs = x_prompt, x_sample
    ca_p, ca_s, cb_p, cb_s, pl_p, pl_s, lat_p, lat_s, kr_p, kr_s = ([] for _ in range(10))
    for l in range(DEPTH):
        lw = {
            'g_pre_mix': g_pre_mix[l], 'w_in': w_in[l], 'conv_a_w': conv_a_w[l], 'conv_a_b': conv_a_b[l],
            'ln_a_g': ln_a_g[l], 'ln_a_b': ln_a_b[l], 'conv_b_w': conv_b_w[l], 'pool_w': pool_w[l],
            'pool_scale': pool_scale[l], 'g_q': g_q[l], 'w_uq': w_uq[l], 'g_kv': g_kv[l], 'w_uk': w_uk[l],
            'w_uv': w_uv[l], 'w_out': w_out[l], 'g_post_mix': g_post_mix[l], 'g_pre_ffn': g_pre_ffn[l],
            'g_post_ffn': g_post_ffn[l], 'g_ple': g_ple[l], 'w_ple_gate': w_ple_gate[l], 'w_ple': w_ple[l],
        }
        if l % 2 == 0:
            j = l // 2
            ffn = functools.partial(swiglu, w_gate=w_ffn_gate[j], w_up=w_ffn_up[j], w_down=w_ffn_down[j])
        else:
            j = l // 2
            ffn = functools.partial(moe_swiglu, w_router=w_router[j], w_gate=w_exp_gate[j],
                                    w_up=w_exp_up[j], w_down=w_exp_down[j])
        xp, na, nb, nc, lat, kr = hybrid_layer(
            xp, p_prompt[l],
            jnp.zeros((B, CONV_A_WIDTH - 1, W_A), dt), jnp.zeros((B, CONV_B_WIDTH - 1, W_B), dt),
            jnp.zeros((B, POOL_BUF, W_C), dt), pos_prompt, 0, mla_prompt_attend, ffn, lw)
        ca_p.append(na); cb_p.append(nb); pl_p.append(nc); lat_p.append(lat); kr_p.append(kr)
        attend_s = functools.partial(mla_sample_attend, cache_latent=cache_latent, cache_krope=cache_krope,
                                     layer=l, page_table=page_table)
        xs, na, nb, nc, lat, kr = hybrid_layer(
            xs, p_sample[l], state_conv_a[l], state_conv_b[l], state_pool[l],
            pos_sample, n_past, attend_s, ffn, lw)
        ca_s.append(na); cb_s.append(nb); pl_s.append(nc); lat_s.append(lat); kr_s.append(kr)
    return (xp, xs,
            jnp.stack(ca_p), jnp.stack(ca_s), jnp.stack(cb_p), jnp.stack(cb_s),
            jnp.stack(pl_p), jnp.stack(pl_s), jnp.stack(lat_p), jnp.stack(lat_s),
            jnp.stack(kr_p), jnp.stack(kr_s))
```

```python
import functools

import jax
import jax.numpy as jnp
from jax import lax
from jax.experimental import pallas as pl
from jax.experimental.pallas import tpu as pltpu

F32 = jnp.float32
BF16 = jnp.bfloat16
EPS = 1e-6
ROPE_BASE = 10000.0
POOL_WINDOWS = (2, 4, 8, 16)
LANE = 128
SUBLANE = 8
VMEM_LIMIT = 56 * 1024 * 1024


def _tile(n, target, align):
    best = None
    d = align
    while d <= min(n, target):
        if n % d == 0:
            best = d
        d += align
    return best if best is not None else n


def _params(*sem):
    return pltpu.CompilerParams(dimension_semantics=sem, vmem_limit_bytes=VMEM_LIMIT)


def _rms(x, g):
    ms = jnp.mean(x * x, axis=-1, keepdims=True)
    return x * lax.rsqrt(ms + EPS) * g


NORM_ROWS = 128


def _row_chunks(n_rows, fn):
    rc = _tile(n_rows, NORM_ROWS, 16)

    def body(r, c):
        fn(pl.ds(pl.multiple_of(r * rc, rc), rc))
        return c

    lax.fori_loop(0, n_rows // rc, body, 0)


def _rms_to_bf16(x_ref, g_ref, xn_ref, norm=True):
    def fn(rows):
        x = x_ref[rows, :].astype(F32)
        xn_ref[rows, :] = (_rms(x, g_ref[...]) if norm else x).astype(BF16)

    _row_chunks(x_ref.shape[0], fn)


def _add_rms_inplace(o_ref, x_ref, g_ref):
    def fn(rows):
        o_ref[rows, :] = x_ref[rows, :] + _rms(o_ref[rows, :], g_ref[...])

    _row_chunks(o_ref.shape[0], fn)


def _rope128(x, c, s1, s2):
    return x * c + pltpu.roll(x, 96, 1) * s1 + pltpu.roll(x, 32, 1) * s2


def _mm_kernel(x_ref, g_ref, w_ref, o_ref, xn_ref, *, norm):
    @pl.when(pl.program_id(1) == 0)
    def _():
        _rms_to_bf16(x_ref, g_ref, xn_ref, norm)

    o_ref[...] = jnp.dot(xn_ref[...], w_ref[...], preferred_element_type=F32).astype(o_ref.dtype)


def _matmul(x, w, g=None, *, rows=None, row0=0, x_col=0, out_dtype=F32, tm_target=640, tn_target=512):
    K, N = w.shape
    rows = x.shape[0] if rows is None else rows
    tm = _tile(rows, tm_target, 16)
    tn = _tile(N, tn_target, LANE)
    assert row0 % tm == 0
    norm = g is not None
    if g is None:
        g = jnp.ones((1, K), F32)
    rb = row0 // tm
    return pl.pallas_call(
        functools.partial(_mm_kernel, norm=norm),
        grid=(rows // tm, N // tn),
        in_specs=[
            pl.BlockSpec((tm, K), lambda i, j: (rb + i, x_col)),
            pl.BlockSpec((1, K), lambda i, j: (0, 0)),
            pl.BlockSpec((K, tn), lambda i, j: (0, j)),
        ],
        out_specs=pl.BlockSpec((tm, tn), lambda i, j: (i, j)),
        out_shape=jax.ShapeDtypeStruct((rows, N), out_dtype),
        scratch_shapes=[pltpu.VMEM((tm, K), BF16)],
        compiler_params=_params("parallel", "arbitrary"),
        name="matmul",
    )(x, g, w)


def _proj_in_kernel(x_ref, g_ref, w_ref, wt_ref, o_ref, ot_ref, xn_ref):
    @pl.when(pl.program_id(1) == 0)
    def _():
        _rms_to_bf16(x_ref, g_ref, xn_ref)
        ot_ref[...] = jnp.dot(xn_ref[...], wt_ref[...], preferred_element_type=F32)

    o_ref[...] = jnp.dot(xn_ref[...], w_ref[...], preferred_element_type=F32)


def _proj_in(x, g, w_main, w_tail):
    M, K = x.shape
    N = w_main.shape[1]
    tm = _tile(M, 640, 16)
    tn = _tile(N, 512, LANE)
    return pl.pallas_call(
        _proj_in_kernel,
        grid=(M // tm, N // tn),
        in_specs=[
            pl.BlockSpec((tm, K), lambda i, j: (i, 0)),
            pl.BlockSpec((1, K), lambda i, j: (0, 0)),
            pl.BlockSpec((K, tn), lambda i, j: (0, j)),
            pl.BlockSpec((K, LANE), lambda i, j: (0, 0)),
        ],
        out_specs=[
            pl.BlockSpec((tm, tn), lambda i, j: (i, j)),
            pl.BlockSpec((tm, LANE), lambda i, j: (i, 0)),
        ],
        out_shape=[
            jax.ShapeDtypeStruct((M, N), F32),
            jax.ShapeDtypeStruct((M, LANE), F32),
        ],
        scratch_shapes=[pltpu.VMEM((tm, K), BF16)],
        compiler_params=_params("parallel", "arbitrary"),
        name="proj_in",
    )(x, g, w_main, w_tail)


def _latent_kernel(ckv_ref, g_ref, kr_ref, c_ref, s1_ref, s2_ref, lat_ref, kpe_ref):
    lat_ref[...] = _rms(ckv_ref[...], g_ref[...])
    kpe_ref[...] = _rope128(kr_ref[...], c_ref[...], s1_ref[...], s2_ref[...])


def _latent(u, ckv_col, kv_rank, g_kv, kpe_raw, tabs):
    M = u.shape[0]
    tm = _tile(M, 640, 16)
    row = lambda i: (i, 0)
    return pl.pallas_call(
        _latent_kernel,
        grid=(M // tm,),
        in_specs=[
            pl.BlockSpec((tm, kv_rank), lambda i: (i, ckv_col)),
            pl.BlockSpec((1, kv_rank), lambda i: (0, 0)),
            pl.BlockSpec((tm, LANE), row),
            pl.BlockSpec((tm, LANE), row),
            pl.BlockSpec((tm, LANE), row),
            pl.BlockSpec((tm, LANE), row),
        ],
        out_specs=[pl.BlockSpec((tm, kv_rank), row), pl.BlockSpec((tm, LANE), row)],
        out_shape=[jax.ShapeDtypeStruct((M, kv_rank), F32), jax.ShapeDtypeStruct((M, LANE), F32)],
        compiler_params=_params("parallel"),
        name="latent",
    )(u, g_kv, kpe_raw, *tabs)


CONV_ROWS = 128


def _mixa_prompt_kernel(val_ref, gate_ref, w_ref, b_ref, y_ref, st_ref, hpad_ref, *, S, KW, pad):
    C = val_ref.shape[1]
    h = val_ref[...] * jax.nn.sigmoid(gate_ref[...])
    hpad_ref[0:pad, :] = jnp.zeros((pad, C), F32)
    hpad_ref[pad:pad + S, :] = h
    st_ref[0] = hpad_ref[pad + S - (KW - 1):pad + S, :]
    off = pad - (KW - 1)
    R = min(CONV_ROWS, S)
    for r in range(S // R):
        acc = jnp.broadcast_to(b_ref[...], (R, C))
        for k in range(KW):
            lo = off + r * R + k
            acc = acc + w_ref[k:k + 1, :] * hpad_ref[lo:lo + R, :]
        y_ref[r * R:(r + 1) * R, :] = acc


def _mixa_prompt(u, B, S, W, conv_w, conv_b):
    KW = conv_w.shape[0]
    pad = -(-(KW - 1) // SUBLANE) * SUBLANE
    C = LANE
    nc = W // C
    return pl.pallas_call(
        functools.partial(_mixa_prompt_kernel, S=S, KW=KW, pad=pad),
        grid=(B, nc),
        in_specs=[
            pl.BlockSpec((S, C), lambda b, c: (b, c)),
            pl.BlockSpec((S, C), lambda b, c: (b, nc + c)),
            pl.BlockSpec((KW, C), lambda b, c: (0, c)),
            pl.BlockSpec((1, C), lambda b, c: (0, c)),
        ],
        out_specs=[
            pl.BlockSpec((S, C), lambda b, c: (b, c)),
            pl.BlockSpec((1, KW - 1, C), lambda b, c: (b, 0, c)),
        ],
        out_shape=[
            jax.ShapeDtypeStruct((B * S, W), F32),
            jax.ShapeDtypeStruct((B, KW - 1, W), F32),
        ],
        scratch_shapes=[pltpu.VMEM((pad + S, C), F32)],
        compiler_params=_params("parallel", "parallel"),
        name="mixa_prompt",
    )(u, u, conv_w, conv_b)


def _ln_silu_kernel(x_ref, g_ref, b_ref, o_ref):
    x = x_ref[...]
    mu = jnp.mean(x, axis=-1, keepdims=True)
    xc = x - mu
    var = jnp.mean(xc * xc, axis=-1, keepdims=True)
    y = xc * lax.rsqrt(var + EPS) * g_ref[...] + b_ref[...]
    o_ref[...] = (y * jax.nn.sigmoid(y)).astype(o_ref.dtype)


def _ln_silu(x, g, b):
    M, W = x.shape
    tm = _tile(M, 512, 16)
    return pl.pallas_call(
        _ln_silu_kernel,
        grid=(M // tm,),
        in_specs=[
            pl.BlockSpec((tm, W), lambda i: (i, 0)),
            pl.BlockSpec((1, W), lambda i: (0, 0)),
            pl.BlockSpec((1, W), lambda i: (0, 0)),
        ],
        out_specs=pl.BlockSpec((tm, W), lambda i: (i, 0)),
        out_shape=jax.ShapeDtypeStruct((M, W), BF16),
        compiler_params=_params("parallel"),
        name="ln_silu",
    )(x, g, b)


def _mixb_prompt_kernel(bb_ref, bc_ref, bx_ref, w_ref, y_ref, st_ref, gpad_ref, *, S, KW, pad):
    C = bb_ref.shape[1]
    g = bc_ref[...] * bx_ref[...]
    gpad_ref[0:pad, :] = jnp.zeros((pad, C), F32)
    gpad_ref[pad:pad + S, :] = g
    st_ref[0] = gpad_ref[pad + S - (KW - 1):pad + S, :]
    off = pad - (KW - 1)
    R = min(CONV_ROWS, S)
    for r in range(S // R):
        acc = jnp.zeros((R, C), F32)
        for k in range(KW):
            lo = off + r * R + k
            acc = acc + w_ref[k:k + 1, :] * gpad_ref[lo:lo + R, :]
        y_ref[r * R:(r + 1) * R, :] = (bb_ref[r * R:(r + 1) * R, :] * acc).astype(y_ref.dtype)


def _mixb_prompt(u, col0, B, S, W, conv_w):
    KW = conv_w.shape[0]
    pad = -(-(KW - 1) // SUBLANE) * SUBLANE
    C = LANE
    nc = W // C
    c0 = col0 // C
    return pl.pallas_call(
        functools.partial(_mixb_prompt_kernel, S=S, KW=KW, pad=pad),
        grid=(B, nc),
        in_specs=[
            pl.BlockSpec((S, C), lambda b, c: (b, c0 + c)),
            pl.BlockSpec((S, C), lambda b, c: (b, c0 + nc + c)),
            pl.BlockSpec((S, C), lambda b, c: (b, c0 + 2 * nc + c)),
            pl.BlockSpec((KW, C), lambda b, c: (0, c)),
        ],
        out_specs=[
            pl.BlockSpec((S, C), lambda b, c: (b, c)),
            pl.BlockSpec((1, KW - 1, C), lambda b, c: (b, 0, c)),
        ],
        out_shape=[
            jax.ShapeDtypeStruct((B * S, W), BF16),
            jax.ShapeDtypeStruct((B, KW - 1, W), F32),
        ],
        scratch_shapes=[pltpu.VMEM((pad + S, C), F32)],
        compiler_params=_params("parallel", "parallel"),
        name="mixb_prompt",
    )(u, u, u, conv_w)


POOL_ROWS = 256


def _mixc_prompt_kernel(v_ref, w_ref, sc_ref, y_ref, st_ref, vpad_ref, *, S, NB):
    C = v_ref.shape[1]
    pad = NB + 1
    grp = pl.program_id(1)
    vpad_ref[0:pad, :] = jnp.zeros((pad, C), F32)
    vpad_ref[pad:pad + S, :] = v_ref[...]
    st_ref[0] = vpad_ref[pad + S - NB:pad + S, :]
    w = w_ref[0]
    R = min(POOL_ROWS, S)
    win = jnp.left_shift(2, grp).astype(F32)
    for r in range(S // R):
        base = pad + r * R
        v = vpad_ref[base:base + R, :]
        acc = v
        sel = None
        for j in range(1, POOL_WINDOWS[-1]):
            acc = acc + vpad_ref[base - j:base - j + R, :]
            if j + 1 in POOL_WINDOWS:
                gi = POOL_WINDOWS.index(j + 1)
                sel = acc if sel is None else jnp.where(grp >= gi, acc, sel)
        pos1 = (lax.broadcasted_iota(jnp.int32, (R, 1), 0) + (r * R + 1)).astype(F32)
        d = sel / jnp.minimum(pos1, win) - v
        y = jnp.dot(d.astype(BF16), w, preferred_element_type=F32) * sc_ref[...]
        y_ref[r * R:(r + 1) * R, :] = y.astype(y_ref.dtype)


def _mixc_prompt(u, col0, B, S, W, pool_w, pool_scale, NB):
    G = pool_w.shape[0]
    C = W // G
    c0 = col0 // C
    assert col0 % C == 0 and G == len(POOL_WINDOWS)
    return pl.pallas_call(
        functools.partial(_mixc_prompt_kernel, S=S, NB=NB),
        grid=(B, G),
        in_specs=[
            pl.BlockSpec((S, C), lambda b, c: (b, c0 + c)),
            pl.BlockSpec((1, C, C), lambda b, c: (c, 0, 0)),
            pl.BlockSpec((1, C), lambda b, c: (0, c)),
        ],
        out_specs=[
            pl.BlockSpec((S, C), lambda b, c: (b, c)),
            pl.BlockSpec((1, NB, C), lambda b, c: (b, 0, c)),
        ],
        out_shape=[
            jax.ShapeDtypeStruct((B * S, W), BF16),
            jax.ShapeDtypeStruct((B, NB, W), F32),
        ],
        scratch_shapes=[pltpu.VMEM((NB + 1 + S, C), F32)],
        compiler_params=_params("parallel", "arbitrary"),
        name="mixc_prompt",
    )(u, pool_w, pool_scale)


def _mix_sample_kernel(av_ref, ag_ref, bb_ref, bc_ref, bx_ref, cv_ref,
                       sa_ref, sb_ref, sc_ref,
                       wa_ref, ba_ref, lg_ref, lb_ref, wb_ref, wp_ref, ps_ref,
                       ya_ref, yb_ref, yc_ref, na_ref, nb_ref, nc_ref, *, cnt):
    KA = wa_ref.shape[0]
    KB = wb_ref.shape[0]
    NB = sc_ref.shape[0]
    h = av_ref[...] * jax.nn.sigmoid(ag_ref[...])
    acc = ba_ref[...] + wa_ref[KA - 1:KA, :] * h
    for k in range(KA - 1):
        acc = acc + wa_ref[k:k + 1, :] * sa_ref[k]
    mu = jnp.mean(acc, axis=-1, keepdims=True)
    xc = acc - mu
    var = jnp.mean(xc * xc, axis=-1, keepdims=True)
    y = xc * lax.rsqrt(var + EPS) * lg_ref[...] + lb_ref[...]
    ya_ref[...] = (y * jax.nn.sigmoid(y)).astype(ya_ref.dtype)
    for k in range(KA - 2):
        na_ref[k] = sa_ref[k + 1]
    na_ref[KA - 2] = h
    g = bc_ref[...] * bx_ref[...]
    accb = wb_ref[KB - 1:KB, :] * g
    for k in range(KB - 1):
        accb = accb + wb_ref[k:k + 1, :] * sb_ref[k]
    yb_ref[...] = (bb_ref[...] * accb).astype(yb_ref.dtype)
    for k in range(KB - 2):
        nb_ref[k] = sb_ref[k + 1]
    nb_ref[KB - 2] = g
    v = cv_ref[...]
    G = wp_ref.shape[0]
    C = wp_ref.shape[1]
    for gi in range(G):
        sl = slice(gi * C, (gi + 1) * C)
        vg = v[:, sl]
        t = vg
        for j in range(1, POOL_WINDOWS[gi]):
            t = t + sc_ref[NB - j][:, sl]
        d = t / cnt[gi] - vg
        yg = jnp.dot(d.astype(BF16), wp_ref[gi], preferred_element_type=F32) * ps_ref[:, sl]
        yc_ref[:, sl] = yg.astype(yc_ref.dtype)
    for k in range(NB - 1):
        nc_ref[k] = sc_ref[k + 1]
    nc_ref[NB - 1] = v


def _mix_sample(u, row0, cols, DB, W, sa, sb, sc, wa, ba, lg, lb, wb, wp, ps, n_past):
    bt = _tile(DB, 16, SUBLANE)
    assert row0 % bt == 0 and all(c % W == 0 for c in cols)
    rb = row0 // bt
    KA, KB, NB = wa.shape[0], wb.shape[0], sc.shape[0]
    cnt = tuple(float(min(n_past + 1, w)) for w in POOL_WINDOWS)
    useg = [pl.BlockSpec((bt, W), functools.partial(lambda i, c: (rb + i, c), c=c // W)) for c in cols]
    st = lambda n: pl.BlockSpec((n, bt, W), lambda i: (0, i, 0))
    full = lambda a: pl.BlockSpec(a.shape, lambda i: (0,) * a.ndim)
    row = pl.BlockSpec((bt, W), lambda i: (i, 0))
    return pl.pallas_call(
        functools.partial(_mix_sample_kernel, cnt=cnt),
        grid=(DB // bt,),
        in_specs=useg + [st(KA - 1), st(KB - 1), st(NB)] + [full(a) for a in (wa, ba, lg, lb, wb, wp, ps)],
        out_specs=[row, row, row, st(KA - 1), st(KB - 1), st(NB)],
        out_shape=[
            jax.ShapeDtypeStruct((DB, W), BF16),
            jax.ShapeDtypeStruct((DB, W), BF16),
            jax.ShapeDtypeStruct((DB, W), BF16),
            jax.ShapeDtypeStruct(sa.shape, F32),
            jax.ShapeDtypeStruct(sb.shape, F32),
            jax.ShapeDtypeStruct(sc.shape, F32),
        ],
        compiler_params=_params("parallel"),
        name="mix_sample",
    )(u, u, u, u, u, u, sa, sb, sc, wa, ba, lg, lb, wb, wp, ps)


def _attn_prompt_kernel(qn_ref, qp_ref, c_ref, s1_ref, s2_ref, kn_ref, kp_ref, v_ref, o_ref, *, tq, tk, scale):
    qi = pl.program_id(2)
    qp = _rope128(qp_ref[...], c_ref[...], s1_ref[...], s2_ref[...])
    q = jnp.concatenate([qn_ref[...], qp], axis=1).astype(BF16)
    row = qi * tq + lax.broadcasted_iota(jnp.int32, (tq, tk), 0)
    col = lax.broadcasted_iota(jnp.int32, (tq, tk), 1)
    dv = v_ref.shape[1]

    def body(j, carry):
        m, l, acc = carry
        ks = pl.multiple_of(j * tk, tk)
        k = jnp.concatenate([kn_ref[pl.ds(ks, tk), :], kp_ref[pl.ds(ks, tk), :]], axis=1)
        s = lax.dot_general(q, k, (((1,), (1,)), ((), ())), preferred_element_type=F32) * scale
        s = jnp.where(col + ks <= row, s, -jnp.inf)
        m_new = jnp.maximum(m, jnp.max(s, axis=-1, keepdims=True))
        p = jnp.exp(s - m_new)
        alpha = jnp.exp(m - m_new)
        l = alpha * l + jnp.sum(p, axis=-1, keepdims=True)
        acc = alpha * acc + jnp.dot(p.astype(BF16), v_ref[pl.ds(ks, tk), :], preferred_element_type=F32)
        return m_new, l, acc

    nk = (qi * tq + tq + tk - 1) // tk
    init = (jnp.full((tq, 1), -jnp.inf, F32), jnp.zeros((tq, 1), F32), jnp.zeros((tq, dv), F32))
    _, l, acc = lax.fori_loop(0, nk, body, init)
    o_ref[...] = (acc / l).astype(o_ref.dtype)


def _attn_prompt(q_all, tabs, kv, kpe_bf, B, S, H, scale):
    tq = _tile(S, 256, 16)
    tk = tq
    nq = S // tq
    qrow = lambda b, h, i: (b * nq + i, 0)
    return pl.pallas_call(
        functools.partial(_attn_prompt_kernel, tq=tq, tk=tk, scale=scale),
        grid=(B, H, nq),
        in_specs=[
            pl.BlockSpec((tq, LANE), lambda b, h, i: (b * nq + i, h)),
            pl.BlockSpec((tq, LANE), lambda b, h, i: (b * nq + i, H + h)),
            pl.BlockSpec((tq, LANE), qrow),
            pl.BlockSpec((tq, LANE), qrow),
            pl.BlockSpec((tq, LANE), qrow),
            pl.BlockSpec((S, LANE), lambda b, h, i: (b, h)),
            pl.BlockSpec((S, LANE), lambda b, h, i: (b, 0)),
            pl.BlockSpec((S, LANE), lambda b, h, i: (b, H + h)),
        ],
        out_specs=pl.BlockSpec((tq, LANE), lambda b, h, i: (b * nq + i, h)),
        out_shape=jax.ShapeDtypeStruct((B * S, H * LANE), BF16),
        compiler_params=_params("parallel", "parallel", "arbitrary"),
        name="attn_prompt",
    )(q_all, q_all, *tabs, kv, kpe_bf, kv)


def _sample_q_kernel(qn_ref, qp_ref, c_ref, s1_ref, s2_ref, wk_ref, ql_ref, qr_ref):
    ql_ref[...] = jnp.dot(qn_ref[...].astype(BF16), wk_ref[0], preferred_element_type=F32).astype(ql_ref.dtype)
    qr_ref[...] = _rope128(qp_ref[...], c_ref[...], s1_ref[...], s2_ref[...])


def _sample_q(q_all, tabs, row0, DB, H, w_ukT):
    KV = w_ukT.shape[2]
    assert row0 % DB == 0
    rb = row0 // DB
    trow = lambda h: (rb, 0)
    return pl.pallas_call(
        _sample_q_kernel,
        grid=(H,),
        in_specs=[
            pl.BlockSpec((DB, LANE), lambda h: (rb, h)),
            pl.BlockSpec((DB, LANE), lambda h: (rb, H + h)),
            pl.BlockSpec((DB, LANE), trow),
            pl.BlockSpec((DB, LANE), trow),
            pl.BlockSpec((DB, LANE), trow),
            pl.BlockSpec((1, LANE, KV), lambda h: (h, 0, 0)),
        ],
        out_specs=[pl.BlockSpec((DB, KV), lambda h: (0, h)), pl.BlockSpec((DB, LANE), lambda h: (0, h))],
        out_shape=[jax.ShapeDtypeStruct((DB, H * KV), BF16), jax.ShapeDtypeStruct((DB, H * LANE), F32)],
        compiler_params=_params("parallel"),
        name="sample_q",
    )(q_all, q_all, *tabs, w_ukT)


PAGES_PER_STEP = 16


def _attn_sample_kernel(pt_ref, ql_ref, qp_ref, ln_ref, kn_ref, lat_hbm, kr_hbm, o_ref,
                        latbuf_ref, krbuf_ref, m_ref, l_ref, acc_ref, sem, *, G, R, layer, scale):
    b = pl.program_id(0)
    j = pl.program_id(1)
    nj = pl.num_programs(1)
    step = b * nj + j
    slot = step % 2
    PS = latbuf_ref.shape[2]

    def page_copies(bb, jj, sl, p):
        page = pt_ref[bb, jj * G + p]
        return (pltpu.make_async_copy(lat_hbm.at[layer, page], latbuf_ref.at[sl, p], sem.at[sl]),
                pltpu.make_async_copy(kr_hbm.at[layer, page], krbuf_ref.at[sl, p], sem.at[sl]))

    def fetch(bb, jj, sl):
        for p in range(G):
            for c in page_copies(bb, jj, sl, p):
                c.start()

    @pl.when(step == 0)
    def _():
        fetch(0, 0, 0)

    @pl.when(step + 1 < pl.num_programs(0) * nj)
    def _():
        last = j == nj - 1
        fetch(jnp.where(last, b + 1, b), jnp.where(last, 0, j + 1), 1 - slot)

    for p in range(G):
        for c in page_copies(b, j, slot, p):
            c.wait()

    ql = ql_ref[0]
    qp = qp_ref[0][:, :R].astype(BF16)

    @pl.when(j == 0)
    def _():
        lat_new = ln_ref[0].astype(BF16).astype(F32)
        kpe_new = kn_ref[0][:, :R].astype(BF16).astype(F32)
        s0 = (jnp.sum(ql.astype(F32) * lat_new, axis=-1, keepdims=True)
              + jnp.sum(qp.astype(F32) * kpe_new, axis=-1, keepdims=True)) * scale
        m_ref[...] = s0
        l_ref[...] = jnp.ones_like(s0)
        acc_ref[...] = jnp.broadcast_to(lat_new, acc_ref.shape)

    nt = (((1,), (1,)), ((), ()))
    lats = [latbuf_ref[slot, p].astype(BF16) for p in range(G)]
    ss = []
    for p in range(G):
        kr = krbuf_ref[slot, p].astype(BF16)
        ss.append(lax.dot_general(ql, lats[p], nt, preferred_element_type=F32)
                  + lax.dot_general(qp, kr, nt, preferred_element_type=F32))
    s = jnp.concatenate(ss, axis=1) * scale
    m = m_ref[...]
    m_new = jnp.maximum(m, jnp.max(s, axis=-1, keepdims=True))
    pr = jnp.exp(s - m_new)
    alpha = jnp.exp(m - m_new)
    l_ref[...] = alpha * l_ref[...] + jnp.sum(pr, axis=-1, keepdims=True)
    pb = pr.astype(BF16)
    pv = jnp.dot(pb[:, 0:PS], lats[0], preferred_element_type=F32)
    for p in range(1, G):
        pv = pv + jnp.dot(pb[:, p * PS:(p + 1) * PS], lats[p], preferred_element_type=F32)
    acc_ref[...] = alpha * acc_ref[...] + pv
    m_ref[...] = m_new

    @pl.when(j == pl.num_programs(1) - 1)
    def _():
        o_ref[0] = acc_ref[...] / l_ref[...]


def _attn_sample(page_table, qlat, qrope, lat_new, kpe_new, cache_latent, cache_krope, layer, scale):
    DB, H, KV = qlat.shape
    n_pages = page_table.shape[1]
    PS = cache_latent.shape[2]
    R = cache_krope.shape[3]
    G = _tile(n_pages, PAGES_PER_STEP, 1)

    per_b = lambda shape: pl.BlockSpec((1,) + shape, lambda b, j, pt: (b, 0, 0))
    grid_spec = pltpu.PrefetchScalarGridSpec(
        num_scalar_prefetch=1,
        grid=(DB, n_pages // G),
        in_specs=[per_b((H, KV)), per_b((H, LANE)), per_b((1, KV)), per_b((1, LANE)),
                  pl.BlockSpec(memory_space=pl.ANY), pl.BlockSpec(memory_space=pl.ANY)],
        out_specs=per_b((H, KV)),
        scratch_shapes=[pltpu.VMEM((2, G, PS, KV), F32), pltpu.VMEM((2, G, PS, R), F32),
                        pltpu.VMEM((H, 1), F32), pltpu.VMEM((H, 1), F32), pltpu.VMEM((H, KV), F32),
                        pltpu.SemaphoreType.DMA((2,))],
    )
    return pl.pallas_call(
        functools.partial(_attn_sample_kernel, G=G, R=R, layer=layer, scale=scale),
        grid_spec=grid_spec,
        out_shape=jax.ShapeDtypeStruct((DB, H, KV), F32),
        compiler_params=_params("arbitrary", "arbitrary"),
        name="attn_sample",
    )(page_table, qlat, qrope, lat_new, kpe_new, cache_latent, cache_krope)


def _sample_out_kernel(o_ref, w_ref, y_ref):
    y_ref[...] = jnp.dot(o_ref[...].astype(BF16), w_ref[0], preferred_element_type=F32).astype(y_ref.dtype)


def _sample_out(o_lat2d, w_uv3):
    DB = o_lat2d.shape[0]
    H, KV, DV = w_uv3.shape
    return pl.pallas_call(
        _sample_out_kernel,
        grid=(H,),
        in_specs=[pl.BlockSpec((DB, KV), lambda h: (0, h)), pl.BlockSpec((1, KV, DV), lambda h: (h, 0, 0))],
        out_specs=pl.BlockSpec((DB, DV), lambda h: (0, h)),
        out_shape=jax.ShapeDtypeStruct((DB, H * DV), BF16),
        compiler_params=_params("parallel"),
        name="sample_out",
    )(o_lat2d, w_uv3)


ACC_COLS = 512


def _accumulate_dot(o_ref, a, w_ref, first):
    N = o_ref.shape[1]
    cw = _tile(N, ACC_COLS, LANE)

    @pl.when(first)
    def _():
        o_ref[...] = jnp.zeros(o_ref.shape, o_ref.dtype)

    for c in range(N // cw):
        sl = slice(c * cw, (c + 1) * cw)
        o_ref[:, sl] += jnp.dot(a, w_ref[:, sl], preferred_element_type=F32)


def _proj_out_kernel(a_ref, w_ref, g_ref, x_ref, o_ref):
    k = pl.program_id(1)
    _accumulate_dot(o_ref, a_ref[...], w_ref, k == 0)

    @pl.when(k == pl.num_programs(1) - 1)
    def _():
        _add_rms_inplace(o_ref, x_ref, g_ref)


def _proj_out(a, w, g, x):
    M, K = a.shape
    N = w.shape[1]
    tm = _tile(M, 640, 16)
    tk = _tile(K, 512, LANE)
    return pl.pallas_call(
        _proj_out_kernel,
        grid=(M // tm, K // tk),
        in_specs=[
            pl.BlockSpec((tm, tk), lambda i, k: (i, k)),
            pl.BlockSpec((tk, N), lambda i, k: (k, 0)),
            pl.BlockSpec((1, N), lambda i, k: (0, 0)),
            pl.BlockSpec((tm, N), lambda i, k: (i, 0), pipeline_mode=pl.Buffered(1)),
        ],
        out_specs=pl.BlockSpec((tm, N), lambda i, k: (i, 0)),
        out_shape=jax.ShapeDtypeStruct((M, N), F32),
        compiler_params=_params("parallel", "arbitrary"),
        name="proj_out",
    )(a, w, g, x)


def _swiglu_hidden(xn, wg, wu):
    g = jnp.dot(xn, wg, preferred_element_type=F32)
    u = jnp.dot(xn, wu, preferred_element_type=F32)
    return (g * jax.nn.sigmoid(g) * u).astype(BF16)


def _ffn_kernel(x_ref, gpre_ref, wg_ref, wu_ref, wd_ref, gpost_ref, o_ref, xn_ref):
    f = pl.program_id(1)

    @pl.when(f == 0)
    def _():
        _rms_to_bf16(x_ref, gpre_ref, xn_ref)

    h = _swiglu_hidden(xn_ref[...], wg_ref[...], wu_ref[...])
    _accumulate_dot(o_ref, h, wd_ref, f == 0)

    @pl.when(f == pl.num_programs(1) - 1)
    def _():
        _add_rms_inplace(o_ref, x_ref, gpost_ref)


def _ffn(x, g_pre, wg, wu, wd, g_post):
    M, D = x.shape
    F = wg.shape[1]
    tm = _tile(M, 640, 16)
    tf = _tile(F, 256, LANE)
    return pl.pallas_call(
        _ffn_kernel,
        grid=(M // tm, F // tf),
        in_specs=[
            pl.BlockSpec((tm, D), lambda i, f: (i, 0), pipeline_mode=pl.Buffered(1)),
            pl.BlockSpec((1, D), lambda i, f: (0, 0)),
            pl.BlockSpec((D, tf), lambda i, f: (0, f)),
            pl.BlockSpec((D, tf), lambda i, f: (0, f)),
            pl.BlockSpec((tf, D), lambda i, f: (f, 0)),
            pl.BlockSpec((1, D), lambda i, f: (0, 0)),
        ],
        out_specs=pl.BlockSpec((tm, D), lambda i, f: (i, 0)),
        out_shape=jax.ShapeDtypeStruct((M, D), F32),
        scratch_shapes=[pltpu.VMEM((tm, D), BF16)],
        compiler_params=_params("parallel", "arbitrary"),
        name="ffn",
    )(x, g_pre, wg, wu, wd, g_post)


def _router_kernel(x_ref, g_ref, w_ref, ti_ref, tg_ref, *, E):
    xn = _rms(x_ref[...], g_ref[...])
    logits = jnp.dot(xn, w_ref[...], preferred_element_type=F32, precision=lax.Precision.HIGHEST)
    lane = lax.broadcasted_iota(jnp.int32, logits.shape, 1)
    logits = jnp.where(lane < E, logits, -jnp.inf)
    mx = jnp.max(logits, axis=-1, keepdims=True)
    ex = jnp.exp(logits - mx)
    probs = ex / jnp.sum(ex, axis=-1, keepdims=True)
    p1 = jnp.max(probs, axis=-1, keepdims=True)
    i1 = jnp.min(jnp.where(probs == p1, lane, LANE), axis=-1, keepdims=True)
    rest = jnp.where(lane == i1, -1.0, probs)
    p2 = jnp.max(rest, axis=-1, keepdims=True)
    i2 = jnp.min(jnp.where(rest == p2, lane, LANE), axis=-1, keepdims=True)
    den = p1 + p2
    ti_ref[...] = jnp.where(lane == 0, i1, jnp.where(lane == 1, i2, 0))
    tg_ref[...] = jnp.where(lane == 0, p1 / den, jnp.where(lane == 1, p2 / den, 0.0))


def _router(x, g, w_router_pad, E):
    M, D = x.shape
    tm = _tile(M, 320, 16)
    return pl.pallas_call(
        functools.partial(_router_kernel, E=E),
        grid=(M // tm,),
        in_specs=[
            pl.BlockSpec((tm, D), lambda i: (i, 0)),
            pl.BlockSpec((1, D), lambda i: (0, 0)),
            pl.BlockSpec((D, LANE), lambda i: (0, 0)),
        ],
        out_specs=[pl.BlockSpec((tm, LANE), lambda i: (i, 0)), pl.BlockSpec((tm, LANE), lambda i: (i, 0))],
        out_shape=[jax.ShapeDtypeStruct((M, LANE), jnp.int32), jax.ShapeDtypeStruct((M, LANE), F32)],
        compiler_params=_params("parallel"),
        name="router",
    )(x, g, w_router_pad)


def _row_gather(src_hbm, dst_ref, sem, n, index_of):
    def issue(r, c):
        pltpu.make_async_copy(src_hbm.at[pl.ds(index_of(r), 1)], dst_ref.at[pl.ds(r, 1)], sem).start()
        return c

    lax.fori_loop(0, n, issue, 0)

    def drain(r, c):
        pltpu.make_async_copy(src_hbm.at[pl.ds(0, 1)], dst_ref.at[pl.ds(r, 1)], sem).wait()
        return c

    lax.fori_loop(0, n, drain, 0)


def _experts_kernel(te_ref, nt_ref, tok_ref, x_hbm, gpre_ref, wg_ref, wu_ref, wd_ref, o_ref,
                    xbuf_ref, xn_ref, sem, *, tm):
    i = pl.program_id(0)
    f = pl.program_id(1)

    @pl.when(i < nt_ref[0])
    def _():
        @pl.when(f == 0)
        def _():
            _row_gather(x_hbm, xbuf_ref, sem, tm, lambda r: tok_ref[i * tm + r])
            _rms_to_bf16(xbuf_ref, gpre_ref, xn_ref)

        h = _swiglu_hidden(xn_ref[...], wg_ref[0], wu_ref[0])
        _accumulate_dot(o_ref, h, wd_ref.at[0], f == 0)

    @pl.when((i >= nt_ref[0]) & (f == 0))
    def _():
        o_ref[...] = jnp.zeros(o_ref.shape, o_ref.dtype)


def _experts(tile_expert, n_tiles, row_token, x, g_pre, wg, wu, wd, tm, n_tiles_max):
    D = x.shape[1]
    F = wg.shape[2]
    tf = _tile(F, 256, LANE)
    nf = F // tf

    def live(i, nt):
        return jnp.minimum(i, nt[0] - 1)

    def fidx(i, f, nt):
        return jnp.where(i < nt[0], f, nf - 1)

    grid_spec = pltpu.PrefetchScalarGridSpec(
        num_scalar_prefetch=3,
        grid=(n_tiles_max, nf),
        in_specs=[
            pl.BlockSpec(memory_space=pl.ANY),
            pl.BlockSpec((1, D), lambda i, f, te, nt, tok: (0, 0)),
            pl.BlockSpec((1, D, tf), lambda i, f, te, nt, tok: (te[live(i, nt)], 0, fidx(i, f, nt))),
            pl.BlockSpec((1, D, tf), lambda i, f, te, nt, tok: (te[live(i, nt)], 0, fidx(i, f, nt))),
            pl.BlockSpec((1, tf, D), lambda i, f, te, nt, tok: (te[live(i, nt)], fidx(i, f, nt), 0)),
        ],
        out_specs=pl.BlockSpec((tm, D), lambda i, f, te, nt, tok: (i, 0)),
        scratch_shapes=[pltpu.VMEM((tm, D), F32), pltpu.VMEM((tm, D), BF16), pltpu.SemaphoreType.DMA(())],
    )
    return pl.pallas_call(
        functools.partial(_experts_kernel, tm=tm),
        grid_spec=grid_spec,
        out_shape=jax.ShapeDtypeStruct((n_tiles_max * tm, D), F32),
        compiler_params=_params("arbitrary", "arbitrary"),
        name="experts",
    )(tile_expert, n_tiles, row_token, x, g_pre, wg, wu, wd)


def _combine_kernel(pos_ref, y_hbm, x_ref, tg_ref, g_ref, o_ref, ybuf_ref, sem, *, tc):
    i = pl.program_id(0)
    _row_gather(y_hbm, ybuf_ref, sem, 2 * tc, lambda r: pos_ref[i * 2 * tc + r])
    tg = tg_ref[...]
    mixed = tg[:, 0:1] * ybuf_ref[0:tc, :] + tg[:, 1:2] * ybuf_ref[tc:2 * tc, :]
    o_ref[...] = x_ref[...] + _rms(mixed, g_ref[...])


def _combine(pos_tiles, y, x, top_g, g_post, tc):
    M, D = x.shape
    grid_spec = pltpu.PrefetchScalarGridSpec(
        num_scalar_prefetch=1,
        grid=(M // tc,),
        in_specs=[
            pl.BlockSpec(memory_space=pl.ANY),
            pl.BlockSpec((tc, D), lambda i, pos: (i, 0)),
            pl.BlockSpec((tc, LANE), lambda i, pos: (i, 0)),
            pl.BlockSpec((1, D), lambda i, pos: (0, 0)),
        ],
        out_specs=pl.BlockSpec((tc, D), lambda i, pos: (i, 0)),
        scratch_shapes=[pltpu.VMEM((2 * tc, D), F32), pltpu.SemaphoreType.DMA(())],
    )
    return pl.pallas_call(
        functools.partial(_combine_kernel, tc=tc),
        grid_spec=grid_spec,
        out_shape=jax.ShapeDtypeStruct((M, D), F32),
        compiler_params=_params("arbitrary"),
        name="combine",
    )(pos_tiles, y, x, top_g, g_post)


def _moe(x, g_pre, w_router, wg, wu, wd, g_post):
    M, D = x.shape
    E = w_router.shape[1]
    wr = jnp.zeros((D, LANE), F32).at[:, :E].set(w_router)
    top_i, top_g = _router(x, g_pre, wr, E)
    tm = _tile(M, 512, 16) if M < 512 else 512
    e_flat = top_i[:, :2].reshape(-1)
    onehot = (e_flat[:, None] == jnp.arange(E, dtype=jnp.int32)[None, :]).astype(jnp.int32)
    counts = jnp.sum(onehot, axis=0)
    rank = jnp.sum((jnp.cumsum(onehot, axis=0) - onehot) * onehot, axis=1)
    padded = ((counts + tm - 1) // tm) * tm
    ends = jnp.cumsum(padded)
    pos = (ends - padded)[e_flat] + rank
    n_tiles_max = -(-2 * M // tm) + E
    row_token = jnp.zeros((n_tiles_max * tm,), jnp.int32).at[pos].set(jnp.arange(2 * M, dtype=jnp.int32) // 2)
    n_tiles = (ends[-1] // tm).astype(jnp.int32).reshape(1)
    tile_start = jnp.arange(n_tiles_max, dtype=jnp.int32) * tm
    tile_expert = jnp.minimum(jnp.searchsorted(ends, tile_start, side="right"), E - 1).astype(jnp.int32)
    y = _experts(tile_expert, n_tiles, row_token, x, g_pre, wg, wu, wd, tm, n_tiles_max)
    tc = _tile(M, 128, SUBLANE)
    pos_tiles = pos.reshape(M // tc, tc, 2).transpose(0, 2, 1).reshape(-1).astype(jnp.int32)
    return _combine(pos_tiles, y, x, top_g, g_post, tc)


def _ple_kernel(x_ref, xc_ref, g_ref, wg_ref, p_ref, wp_ref, o_ref, xn_ref):
    @pl.when(pl.program_id(1) == 0)
    def _():
        _rms_to_bf16(x_ref, g_ref, xn_ref)

    gate = jax.nn.sigmoid(jnp.dot(xn_ref[...], wg_ref[...], preferred_element_type=F32))
    emb = jnp.dot(p_ref[...].astype(BF16), wp_ref[...], preferred_element_type=F32)
    o_ref[...] = xc_ref[...] + gate * emb


def _ple(x, g, w_gate, p, w_ple):
    M, D = x.shape
    P = p.shape[1]
    tm = _tile(M, 640, 16)
    tn = _tile(D, 512, LANE)
    return pl.pallas_call(
        _ple_kernel,
        grid=(M // tm, D // tn),
        in_specs=[
            pl.BlockSpec((tm, D), lambda i, j: (i, 0)),
            pl.BlockSpec((tm, tn), lambda i, j: (i, j)),
            pl.BlockSpec((1, D), lambda i, j: (0, 0)),
            pl.BlockSpec((D, tn), lambda i, j: (0, j)),
            pl.BlockSpec((tm, P), lambda i, j: (i, 0)),
            pl.BlockSpec((P, tn), lambda i, j: (0, j)),
        ],
        out_specs=pl.BlockSpec((tm, tn), lambda i, j: (i, j)),
        out_shape=jax.ShapeDtypeStruct((M, D), F32),
        scratch_shapes=[pltpu.VMEM((tm, D), BF16)],
        compiler_params=_params("parallel", "arbitrary"),
        name="ple",
    )(x, x, g, w_gate, p, w_ple)


def _rope_tables(pos):
    half = 32
    inv = ROPE_BASE ** (-jnp.arange(0, 2 * half, 2, dtype=F32) / (2 * half))
    ang = pos.astype(F32)[:, None] * inv[None, :]
    cos, sin = jnp.cos(ang), jnp.sin(ang)
    z = jnp.zeros_like(cos)
    return (jnp.concatenate([cos, cos, z, z], axis=1),
            jnp.concatenate([-sin, z, z, z], axis=1),
            jnp.concatenate([z, sin, z, z], axis=1))


def kernel(x_prompt, x_sample, p_prompt, p_sample, state_conv_a, state_conv_b, state_pool, cache_latent, cache_krope, page_table, g_pre_mix, w_in, conv_a_w, conv_a_b, ln_a_g, ln_a_b, conv_b_w, pool_w, pool_scale, g_q, w_uq, g_kv, w_uk, w_uv, w_out, g_post_mix, g_pre_ffn, g_post_ffn, w_ffn_gate, w_ffn_up, w_ffn_down, w_router, w_exp_gate, w_exp_up, w_exp_down, g_ple, w_ple_gate, w_ple):
    B, S, D = x_prompt.shape
    DB, T, _ = x_sample.shape
    assert T == 1
    depth = w_in.shape[0]
    Np = B * S
    M = Np + DB * T
    W_A = conv_a_w.shape[2]
    W_B = conv_b_w.shape[2]
    W_C = pool_scale.shape[1]
    assert W_A == W_B == W_C
    Q_RANK = g_q.shape[1]
    KV = g_kv.shape[1]
    H = w_uq.shape[2]
    NOPE = w_uk.shape[3]
    ROPE = cache_krope.shape[3]
    DV = w_uv.shape[3]
    assert NOPE == LANE and DV == LANE and ROPE == 64 and w_uq.shape[3] == NOPE + ROPE
    NB = state_pool.shape[2]
    n_past = page_table.shape[1] * cache_latent.shape[2]
    scale = float(NOPE + ROPE) ** -0.5
    N_main = 2 * W_A + 3 * W_B + W_C + Q_RANK + KV
    assert w_in.shape[2] == N_main + ROPE
    col_a, col_b, col_c = 0, 2 * W_A, 2 * W_A + 3 * W_B
    col_q = col_c + W_C
    col_kv = col_q + Q_RANK
    assert col_q % Q_RANK == 0 and col_kv % KV == 0

    x = jnp.concatenate([x_prompt.reshape(Np, D), x_sample.reshape(DB, D)], axis=0)
    pos = jnp.concatenate([jnp.tile(jnp.arange(S), B), jnp.full((DB,), n_past)])
    tabs = _rope_tables(pos)
    row2 = lambda a: a.reshape(1, -1)
    outs = [[] for _ in range(10)]

    for l in range(depth):
        w_main = w_in[l, :, :N_main].astype(BF16)
        w_tail = jnp.pad(w_in[l, :, N_main:], ((0, 0), (0, LANE - ROPE))).astype(BF16)
        wq = w_uq[l]
        wq_r = jnp.concatenate(
            [wq[:, :, :NOPE].reshape(Q_RANK, H * NOPE),
             jnp.pad(wq[:, :, NOPE:], ((0, 0), (0, 0), (0, LANE - ROPE))).reshape(Q_RANK, H * LANE)],
            axis=1).astype(BF16)
        w_kv = jnp.concatenate([w_uk[l].reshape(KV, H * NOPE), w_uv[l].reshape(KV, H * DV)], axis=1).astype(BF16)
        w_ukT = jnp.transpose(w_uk[l], (1, 2, 0)).astype(BF16)
        w_uv3 = jnp.transpose(w_uv[l], (1, 0, 2)).astype(BF16)

        u, kpe_raw = _proj_in(x, row2(g_pre_mix[l]), w_main, w_tail)
        latent, kpe = _latent(u, col_kv // KV, KV, row2(g_kv[l]), kpe_raw, tabs)
        q_all = _matmul(u, wq_r, row2(g_q[l]), x_col=col_q // Q_RANK)

        ya_pre, na_p = _mixa_prompt(u, B, S, W_A, conv_a_w[l], row2(conv_a_b[l]))
        ya_p = _ln_silu(ya_pre, row2(ln_a_g[l]), row2(ln_a_b[l]))
        yb_p, nb_p = _mixb_prompt(u, col_b, B, S, W_B, conv_b_w[l])
        pw = pool_w[l].astype(BF16)
        yc_p, nc_p = _mixc_prompt(u, col_c, B, S, W_C, pw, row2(pool_scale[l]), NB)
        kv = _matmul(latent, w_kv, rows=Np, out_dtype=BF16)
        yd_p = _attn_prompt(q_all, tabs, kv, kpe.astype(BF16), B, S, H, scale)

        sa = jnp.transpose(state_conv_a[l], (1, 0, 2))
        sb = jnp.transpose(state_conv_b[l], (1, 0, 2))
        sc = jnp.transpose(state_pool[l], (1, 0, 2))
        cols = (col_a, col_a + W_A, col_b, col_b + W_B, col_b + 2 * W_B, col_c)
        ya_s, yb_s, yc_s, na_s, nb_s, nc_s = _mix_sample(
            u, Np, cols, DB, W_A, sa, sb, sc, conv_a_w[l], row2(conv_a_b[l]), row2(ln_a_g[l]),
            row2(ln_a_b[l]), conv_b_w[l], pw, row2(pool_scale[l]), n_past)
        qlat, qrope = _sample_q(q_all, tabs, Np, DB, H, w_ukT)
        o_lat = _attn_sample(page_table, qlat.reshape(DB, H, KV), qrope.reshape(DB, H, LANE),
                             latent[Np:].reshape(DB, 1, KV), kpe[Np:].reshape(DB, 1, LANE),
                             cache_latent, cache_krope, l, scale)
        yd_s = _sample_out(o_lat.reshape(DB, H * KV), w_uv3)

        mix = jnp.concatenate([jnp.concatenate([ya_p, yb_p, yc_p, yd_p], axis=1),
                               jnp.concatenate([ya_s, yb_s, yc_s, yd_s], axis=1)], axis=0)
        x = _proj_out(mix, w_out[l].astype(BF16), row2(g_post_mix[l]), x)

        j = l // 2
        if l % 2 == 0:
            x = _ffn(x, row2(g_pre_ffn[l]), w_ffn_gate[j].astype(BF16), w_ffn_up[j].astype(BF16),
                     w_ffn_down[j].astype(BF16), row2(g_post_ffn[l]))
        else:
            x = _moe(x, row2(g_pre_ffn[l]), w_router[j], w_exp_gate[j].astype(BF16),
                     w_exp_up[j].astype(BF16), w_exp_down[j].astype(BF16), row2(g_post_ffn[l]))

        p_l = jnp.concatenate([p_prompt[l].reshape(Np, -1), p_sample[l].reshape(DB, -1)], axis=0)
        x = _ple(x, row2(g_ple[l]), w_ple_gate[l].astype(BF16), p_l, w_ple[l].astype(BF16))

        new = (na_p, jnp.transpose(na_s, (1, 0, 2)), nb_p, jnp.transpose(nb_s, (1, 0, 2)),
               nc_p, jnp.transpose(nc_s, (1, 0, 2)),
               latent[:Np].reshape(B, S, KV), latent[Np:].reshape(DB, T, KV),
               kpe[:Np, :ROPE].reshape(B, S, ROPE), kpe[Np:, :ROPE].reshape(DB, T, ROPE))
        for o, v in zip(outs, new):
            o.append(v)

    return (x[:Np].reshape(B, S, D), x[Np:].reshape(DB, T, D)) + tuple(jnp.stack(o) for o in outs)
```

```python
import functools

import jax
import jax.numpy as jnp
from jax import lax
from jax.experimental import pallas as pl
from jax.experimental.pallas import tpu as pltpu

F32 = jnp.float32
BF16 = jnp.bfloat16
EPS = 1e-6
ROPE_BASE = 10000.0
POOL_WINDOWS = (2, 4, 8, 16)
LANE = 128
SUBLANE = 8
VMEM_LIMIT = 56 * 1024 * 1024


def _tile(n, target, align):
    best = None
    d = align
    while d <= min(n, target):
        if n % d == 0:
            best = d
        d += align
    return best if best is not None else n


def _params(*sem):
    return pltpu.CompilerParams(dimension_semantics=sem, vmem_limit_bytes=VMEM_LIMIT)


def _rms(x, g):
    ms = jnp.mean(x * x, axis=-1, keepdims=True)
    return x * lax.rsqrt(ms + EPS) * g


NORM_ROWS = 128


def _row_chunks(n_rows, fn):
    rc = _tile(n_rows, NORM_ROWS, 16)

    def body(r, c):
        fn(pl.ds(pl.multiple_of(r * rc, rc), rc))
        return c

    lax.fori_loop(0, n_rows // rc, body, 0)


def _rms_to_bf16(x_ref, g_ref, xn_ref, norm=True):
    def fn(rows):
        x = x_ref[rows, :].astype(F32)
        xn_ref[rows, :] = (_rms(x, g_ref[...]) if norm else x).astype(BF16)

    _row_chunks(x_ref.shape[0], fn)


def _add_rms_inplace(o_ref, x_ref, g_ref):
    def fn(rows):
        o_ref[rows, :] = x_ref[rows, :] + _rms(o_ref[rows, :], g_ref[...])

    _row_chunks(o_ref.shape[0], fn)


def _rope128(x, c, s1, s2):
    return x * c + pltpu.roll(x, 96, 1) * s1 + pltpu.roll(x, 32, 1) * s2


def _mm_kernel(x_ref, g_ref, w_ref, o_ref, xn_ref, *, norm):
    @pl.when(pl.program_id(1) == 0)
    def _():
        _rms_to_bf16(x_ref, g_ref, xn_ref, norm)

    o_ref[...] = jnp.dot(xn_ref[...], w_ref[...], preferred_element_type=F32).astype(o_ref.dtype)


def _matmul(x, w, g=None, *, rows=None, row0=0, x_col=0, out_dtype=F32, tm_target=640, tn_target=512):
    K, N = w.shape
    rows = x.shape[0] if rows is None else rows
    tm = _tile(rows, tm_target, 16)
    tn = _tile(N, tn_target, LANE)
    assert row0 % tm == 0
    norm = g is not None
    if g is None:
        g = jnp.ones((1, K), F32)
    rb = row0 // tm
    return pl.pallas_call(
        functools.partial(_mm_kernel, norm=norm),
        grid=(rows // tm, N // tn),
        in_specs=[
            pl.BlockSpec((tm, K), lambda i, j: (rb + i, x_col)),
            pl.BlockSpec((1, K), lambda i, j: (0, 0)),
            pl.BlockSpec((K, tn), lambda i, j: (0, j)),
        ],
        out_specs=pl.BlockSpec((tm, tn), lambda i, j: (i, j)),
        out_shape=jax.ShapeDtypeStruct((rows, N), out_dtype),
        scratch_shapes=[pltpu.VMEM((tm, K), BF16)],
        compiler_params=_params("parallel", "arbitrary"),
        name="matmul",
    )(x, g, w)


def _proj_in_kernel(x_ref, g_ref, w_ref, wt_ref, o_ref, ot_ref, xn_ref):
    @pl.when(pl.program_id(1) == 0)
    def _():
        _rms_to_bf16(x_ref, g_ref, xn_ref)
        ot_ref[...] = jnp.dot(xn_ref[...], wt_ref[...], preferred_element_type=F32)

    o_ref[...] = jnp.dot(xn_ref[...], w_ref[...], preferred_element_type=F32)


def _proj_in(x, g, w_main, w_tail):
    M, K = x.shape
    N = w_main.shape[1]
    tm = _tile(M, 640, 16)
    tn = _tile(N, 768, LANE)
    return pl.pallas_call(
        _proj_in_kernel,
        grid=(M // tm, N // tn),
        in_specs=[
            pl.BlockSpec((tm, K), lambda i, j: (i, 0)),
            pl.BlockSpec((1, K), lambda i, j: (0, 0)),
            pl.BlockSpec((K, tn), lambda i, j: (0, j)),
            pl.BlockSpec((K, LANE), lambda i, j: (0, 0)),
        ],
        out_specs=[
            pl.BlockSpec((tm, tn), lambda i, j: (i, j)),
            pl.BlockSpec((tm, LANE), lambda i, j: (i, 0)),
        ],
        out_shape=[
            jax.ShapeDtypeStruct((M, N), F32),
            jax.ShapeDtypeStruct((M, LANE), F32),
        ],
        scratch_shapes=[pltpu.VMEM((tm, K), BF16)],
        compiler_params=_params("parallel", "arbitrary"),
        name="proj_in",
    )(x, g, w_main, w_tail)


def _latent_kernel(ckv_ref, g_ref, kr_ref, c_ref, s1_ref, s2_ref, lat_ref, kpe_ref):
    lat_ref[...] = _rms(ckv_ref[...], g_ref[...])
    kpe_ref[...] = _rope128(kr_ref[...], c_ref[...], s1_ref[...], s2_ref[...])


def _latent(u, ckv_col, kv_rank, g_kv, kpe_raw, tabs):
    M = u.shape[0]
    tm = _tile(M, 640, 16)
    row = lambda i: (i, 0)
    return pl.pallas_call(
        _latent_kernel,
        grid=(M // tm,),
        in_specs=[
            pl.BlockSpec((tm, kv_rank), lambda i: (i, ckv_col)),
            pl.BlockSpec((1, kv_rank), lambda i: (0, 0)),
            pl.BlockSpec((tm, LANE), row),
            pl.BlockSpec((tm, LANE), row),
            pl.BlockSpec((tm, LANE), row),
            pl.BlockSpec((tm, LANE), row),
        ],
        out_specs=[pl.BlockSpec((tm, kv_rank), row), pl.BlockSpec((tm, LANE), row)],
        out_shape=[jax.ShapeDtypeStruct((M, kv_rank), F32), jax.ShapeDtypeStruct((M, LANE), F32)],
        compiler_params=_params("parallel"),
        name="latent",
    )(u, g_kv, kpe_raw, *tabs)


CONV_ROWS = 128


def _mixa_prompt_kernel(val_ref, gate_ref, w_ref, b_ref, y_ref, st_ref, hpad_ref, *, S, KW, pad):
    C = val_ref.shape[1]
    h = val_ref[...] * jax.nn.sigmoid(gate_ref[...])
    hpad_ref[0:pad, :] = jnp.zeros((pad, C), F32)
    hpad_ref[pad:pad + S, :] = h
    st_ref[0] = hpad_ref[pad + S - (KW - 1):pad + S, :]
    off = pad - (KW - 1)
    R = min(CONV_ROWS, S)
    for r in range(S // R):
        acc = jnp.broadcast_to(b_ref[...], (R, C))
        for k in range(KW):
            lo = off + r * R + k
            acc = acc + w_ref[k:k + 1, :] * hpad_ref[lo:lo + R, :]
        y_ref[r * R:(r + 1) * R, :] = acc


def _mixa_prompt(u, B, S, W, conv_w, conv_b):
    KW = conv_w.shape[0]
    pad = -(-(KW - 1) // SUBLANE) * SUBLANE
    C = LANE
    nc = W // C
    return pl.pallas_call(
        functools.partial(_mixa_prompt_kernel, S=S, KW=KW, pad=pad),
        grid=(B, nc),
        in_specs=[
            pl.BlockSpec((S, C), lambda b, c: (b, c)),
            pl.BlockSpec((S, C), lambda b, c: (b, nc + c)),
            pl.BlockSpec((KW, C), lambda b, c: (0, c)),
            pl.BlockSpec((1, C), lambda b, c: (0, c)),
        ],
        out_specs=[
            pl.BlockSpec((S, C), lambda b, c: (b, c)),
            pl.BlockSpec((1, KW - 1, C), lambda b, c: (b, 0, c)),
        ],
        out_shape=[
            jax.ShapeDtypeStruct((B * S, W), F32),
            jax.ShapeDtypeStruct((B, KW - 1, W), F32),
        ],
        scratch_shapes=[pltpu.VMEM((pad + S, C), F32)],
        compiler_params=_params("parallel", "parallel"),
        name="mixa_prompt",
    )(u, u, conv_w, conv_b)


def _ln_silu_kernel(x_ref, g_ref, b_ref, o_ref):
    x = x_ref[...]
    mu = jnp.mean(x, axis=-1, keepdims=True)
    xc = x - mu
    var = jnp.mean(xc * xc, axis=-1, keepdims=True)
    y = xc * lax.rsqrt(var + EPS) * g_ref[...] + b_ref[...]
    o_ref[...] = (y * jax.nn.sigmoid(y)).astype(o_ref.dtype)


def _ln_silu(x, g, b):
    M, W = x.shape
    tm = _tile(M, 512, 16)
    return pl.pallas_call(
        _ln_silu_kernel,
        grid=(M // tm,),
        in_specs=[
            pl.BlockSpec((tm, W), lambda i: (i, 0)),
            pl.BlockSpec((1, W), lambda i: (0, 0)),
            pl.BlockSpec((1, W), lambda i: (0, 0)),
        ],
        out_specs=pl.BlockSpec((tm, W), lambda i: (i, 0)),
        out_shape=jax.ShapeDtypeStruct((M, W), BF16),
        compiler_params=_params("parallel"),
        name="ln_silu",
    )(x, g, b)


def _mixb_prompt_kernel(bb_ref, bc_ref, bx_ref, w_ref, y_ref, st_ref, gpad_ref, *, S, KW, pad):
    C = bb_ref.shape[1]
    g = bc_ref[...] * bx_ref[...]
    gpad_ref[0:pad, :] = jnp.zeros((pad, C), F32)
    gpad_ref[pad:pad + S, :] = g
    st_ref[0] = gpad_ref[pad + S - (KW - 1):pad + S, :]
    off = pad - (KW - 1)
    R = min(CONV_ROWS, S)
    for r in range(S // R):
        acc = jnp.zeros((R, C), F32)
        for k in range(KW):
            lo = off + r * R + k
            acc = acc + w_ref[k:k + 1, :] * gpad_ref[lo:lo + R, :]
        y_ref[r * R:(r + 1) * R, :] = (bb_ref[r * R:(r + 1) * R, :] * acc).astype(y_ref.dtype)


def _mixb_prompt(u, col0, B, S, W, conv_w):
    KW = conv_w.shape[0]
    pad = -(-(KW - 1) // SUBLANE) * SUBLANE
    C = LANE
    nc = W // C
    c0 = col0 // C
    return pl.pallas_call(
        functools.partial(_mixb_prompt_kernel, S=S, KW=KW, pad=pad),
        grid=(B, nc),
        in_specs=[
            pl.BlockSpec((S, C), lambda b, c: (b, c0 + c)),
            pl.BlockSpec((S, C), lambda b, c: (b, c0 + nc + c)),
            pl.BlockSpec((S, C), lambda b, c: (b, c0 + 2 * nc + c)),
            pl.BlockSpec((KW, C), lambda b, c: (0, c)),
        ],
        out_specs=[
            pl.BlockSpec((S, C), lambda b, c: (b, c)),
            pl.BlockSpec((1, KW - 1, C), lambda b, c: (b, 0, c)),
        ],
        out_shape=[
            jax.ShapeDtypeStruct((B * S, W), BF16),
            jax.ShapeDtypeStruct((B, KW - 1, W), F32),
        ],
        scratch_shapes=[pltpu.VMEM((pad + S, C), F32)],
        compiler_params=_params("parallel", "parallel"),
        name="mixb_prompt",
    )(u, u, u, conv_w)


POOL_ROWS = 256


def _mixc_prompt_kernel(v_ref, w_ref, sc_ref, y_ref, st_ref, vpad_ref, *, S, NB):
    C = v_ref.shape[1]
    pad = NB + 1
    grp = pl.program_id(1)
    vpad_ref[0:pad, :] = jnp.zeros((pad, C), F32)
    vpad_ref[pad:pad + S, :] = v_ref[...]
    st_ref[0] = vpad_ref[pad + S - NB:pad + S, :]
    w = w_ref[0]
    R = min(POOL_ROWS, S)
    win = jnp.left_shift(2, grp).astype(F32)
    for r in range(S // R):
        base = pad + r * R
        v = vpad_ref[base:base + R, :]
        acc = v
        sel = None
        for j in range(1, POOL_WINDOWS[-1]):
            acc = acc + vpad_ref[base - j:base - j + R, :]
            if j + 1 in POOL_WINDOWS:
                gi = POOL_WINDOWS.index(j + 1)
                sel = acc if sel is None else jnp.where(grp >= gi, acc, sel)
        pos1 = (lax.broadcasted_iota(jnp.int32, (R, 1), 0) + (r * R + 1)).astype(F32)
        d = sel / jnp.minimum(pos1, win) - v
        y = jnp.dot(d.astype(BF16), w, preferred_element_type=F32) * sc_ref[...]
        y_ref[r * R:(r + 1) * R, :] = y.astype(y_ref.dtype)


def _mixc_prompt(u, col0, B, S, W, pool_w, pool_scale, NB):
    G = pool_w.shape[0]
    C = W // G
    c0 = col0 // C
    assert col0 % C == 0 and G == len(POOL_WINDOWS)
    return pl.pallas_call(
        functools.partial(_mixc_prompt_kernel, S=S, NB=NB),
        grid=(B, G),
        in_specs=[
            pl.BlockSpec((S, C), lambda b, c: (b, c0 + c)),
            pl.BlockSpec((1, C, C), lambda b, c: (c, 0, 0)),
            pl.BlockSpec((1, C), lambda b, c: (0, c)),
        ],
        out_specs=[
            pl.BlockSpec((S, C), lambda b, c: (b, c)),
            pl.BlockSpec((1, NB, C), lambda b, c: (b, 0, c)),
        ],
        out_shape=[
            jax.ShapeDtypeStruct((B * S, W), BF16),
            jax.ShapeDtypeStruct((B, NB, W), F32),
        ],
        scratch_shapes=[pltpu.VMEM((NB + 1 + S, C), F32)],
        compiler_params=_params("parallel", "arbitrary"),
        name="mixc_prompt",
    )(u, pool_w, pool_scale)


def _mix_sample_kernel(av_ref, ag_ref, bb_ref, bc_ref, bx_ref, cv_ref,
                       sa_ref, sb_ref, sc_ref,
                       wa_ref, ba_ref, lg_ref, lb_ref, wb_ref, wp_ref, ps_ref,
                       ya_ref, yb_ref, yc_ref, na_ref, nb_ref, nc_ref, *, cnt):
    KA = wa_ref.shape[0]
    KB = wb_ref.shape[0]
    NB = sc_ref.shape[0]
    h = av_ref[...] * jax.nn.sigmoid(ag_ref[...])
    acc = ba_ref[...] + wa_ref[KA - 1:KA, :] * h
    for k in range(KA - 1):
        acc = acc + wa_ref[k:k + 1, :] * sa_ref[k]
    mu = jnp.mean(acc, axis=-1, keepdims=True)
    xc = acc - mu
    var = jnp.mean(xc * xc, axis=-1, keepdims=True)
    y = xc * lax.rsqrt(var + EPS) * lg_ref[...] + lb_ref[...]
    ya_ref[...] = (y * jax.nn.sigmoid(y)).astype(ya_ref.dtype)
    for k in range(KA - 2):
        na_ref[k] = sa_ref[k + 1]
    na_ref[KA - 2] = h
    g = bc_ref[...] * bx_ref[...]
    accb = wb_ref[KB - 1:KB, :] * g
    for k in range(KB - 1):
        accb = accb + wb_ref[k:k + 1, :] * sb_ref[k]
    yb_ref[...] = (bb_ref[...] * accb).astype(yb_ref.dtype)
    for k in range(KB - 2):
        nb_ref[k] = sb_ref[k + 1]
    nb_ref[KB - 2] = g
    v = cv_ref[...]
    G = wp_ref.shape[0]
    C = wp_ref.shape[1]
    for gi in range(G):
        sl = slice(gi * C, (gi + 1) * C)
        vg = v[:, sl]
        t = vg
        for j in range(1, POOL_WINDOWS[gi]):
            t = t + sc_ref[NB - j][:, sl]
        d = t / cnt[gi] - vg
        yg = jnp.dot(d.astype(BF16), wp_ref[gi], preferred_element_type=F32) * ps_ref[:, sl]
        yc_ref[:, sl] = yg.astype(yc_ref.dtype)
    for k in range(NB - 1):
        nc_ref[k] = sc_ref[k + 1]
    nc_ref[NB - 1] = v


def _mix_sample(u, row0, cols, DB, W, sa, sb, sc, wa, ba, lg, lb, wb, wp, ps, n_past):
    bt = _tile(DB, 16, SUBLANE)
    assert row0 % bt == 0 and all(c % W == 0 for c in cols)
    rb = row0 // bt
    KA, KB, NB = wa.shape[0], wb.shape[0], sc.shape[0]
    cnt = tuple(float(min(n_past + 1, w)) for w in POOL_WINDOWS)
    useg = [pl.BlockSpec((bt, W), functools.partial(lambda i, c: (rb + i, c), c=c // W)) for c in cols]
    st = lambda n: pl.BlockSpec((n, bt, W), lambda i: (0, i, 0))
    full = lambda a: pl.BlockSpec(a.shape, lambda i: (0,) * a.ndim)
    row = pl.BlockSpec((bt, W), lambda i: (i, 0))
    return pl.pallas_call(
        functools.partial(_mix_sample_kernel, cnt=cnt),
        grid=(DB // bt,),
        in_specs=useg + [st(KA - 1), st(KB - 1), st(NB)] + [full(a) for a in (wa, ba, lg, lb, wb, wp, ps)],
        out_specs=[row, row, row, st(KA - 1), st(KB - 1), st(NB)],
        out_shape=[
            jax.ShapeDtypeStruct((DB, W), BF16),
            jax.ShapeDtypeStruct((DB, W), BF16),
            jax.ShapeDtypeStruct((DB, W), BF16),
            jax.ShapeDtypeStruct(sa.shape, F32),
            jax.ShapeDtypeStruct(sb.shape, F32),
            jax.ShapeDtypeStruct(sc.shape, F32),
        ],
        compiler_params=_params("parallel"),
        name="mix_sample",
    )(u, u, u, u, u, u, sa, sb, sc, wa, ba, lg, lb, wb, wp, ps)


ATTN_TILE = 512
HEADS_PER_STEP = 2


def _attn_prompt_kernel(qn_ref, qp_ref, c_ref, s1_ref, s2_ref, kn_ref, kp_ref, v_ref, o_ref,
                        q_scr, m_scr, l_scr, acc_scr, *, t, HB, scale):
    qi = pl.program_id(2)
    c, s1, s2 = c_ref[...], s1_ref[...], s2_ref[...]
    for hh in range(HB):
        hs = slice(hh * LANE, (hh + 1) * LANE)
        qp = _rope128(qp_ref[:, hs], c, s1, s2)
        q_scr[hh] = jnp.concatenate([qn_ref[:, hs], qp], axis=1).astype(BF16)
        m_scr[hh] = jnp.full((t, 1), -jnp.inf, F32)
        l_scr[hh] = jnp.zeros((t, 1), F32)
        acc_scr[hh] = jnp.zeros((t, LANE), F32)

    def tile(j, masked):
        ks = pl.multiple_of(j * t, t)
        kp = kp_ref[pl.ds(ks, t), :]
        for hh in range(HB):
            hs = slice(hh * LANE, (hh + 1) * LANE)
            k = jnp.concatenate([kn_ref[pl.ds(ks, t), hs], kp], axis=1)
            s = lax.dot_general(q_scr[hh], k, (((1,), (1,)), ((), ())), preferred_element_type=F32) * scale
            if masked:
                row = lax.broadcasted_iota(jnp.int32, (t, t), 0)
                col = lax.broadcasted_iota(jnp.int32, (t, t), 1)
                s = jnp.where(col <= row, s, -jnp.inf)
            m = m_scr[hh]
            m_new = jnp.maximum(m, jnp.max(s, axis=-1, keepdims=True))
            p = jnp.exp(s - m_new)
            alpha = jnp.exp(m - m_new)
            l_scr[hh] = alpha * l_scr[hh] + jnp.sum(p, axis=-1, keepdims=True)
            acc_scr[hh] = alpha * acc_scr[hh] + jnp.dot(p.astype(BF16), v_ref[pl.ds(ks, t), hs],
                                                        preferred_element_type=F32)
            m_scr[hh] = m_new

    def body(j, carry):
        tile(j, False)
        return carry

    lax.fori_loop(0, qi, body, 0)
    tile(qi, True)
    for hh in range(HB):
        o_ref[:, hh * LANE:(hh + 1) * LANE] = (acc_scr[hh] / l_scr[hh]).astype(o_ref.dtype)


def _attn_prompt(q_all, tabs, kv, kpe_bf, B, S, H, scale):
    t = _tile(S, ATTN_TILE, 16)
    nq = S // t
    HB = HEADS_PER_STEP if H % HEADS_PER_STEP == 0 else 1
    HG = H // HB
    W = HB * LANE
    qrow = lambda b, h, i: (b * nq + i, 0)
    return pl.pallas_call(
        functools.partial(_attn_prompt_kernel, t=t, HB=HB, scale=scale),
        grid=(B, HG, nq),
        in_specs=[
            pl.BlockSpec((t, W), lambda b, h, i: (b * nq + i, h)),
            pl.BlockSpec((t, W), lambda b, h, i: (b * nq + i, HG + h)),
            pl.BlockSpec((t, LANE), qrow),
            pl.BlockSpec((t, LANE), qrow),
            pl.BlockSpec((t, LANE), qrow),
            pl.BlockSpec((S, W), lambda b, h, i: (b, h)),
            pl.BlockSpec((S, LANE), lambda b, h, i: (b, 0)),
            pl.BlockSpec((S, W), lambda b, h, i: (b, HG + h)),
        ],
        out_specs=pl.BlockSpec((t, W), lambda b, h, i: (b * nq + i, h)),
        out_shape=jax.ShapeDtypeStruct((B * S, H * LANE), BF16),
        scratch_shapes=[pltpu.VMEM((HB, t, 2 * LANE), BF16), pltpu.VMEM((HB, t, 1), F32),
                        pltpu.VMEM((HB, t, 1), F32), pltpu.VMEM((HB, t, LANE), F32)],
        compiler_params=_params("parallel", "parallel", "arbitrary"),
        name="attn_prompt",
    )(q_all, q_all, *tabs, kv, kpe_bf, kv)


def _sample_q_kernel(qn_ref, qp_ref, c_ref, s1_ref, s2_ref, wk_ref, ql_ref, qr_ref):
    ql_ref[...] = jnp.dot(qn_ref[...].astype(BF16), wk_ref[0], preferred_element_type=F32).astype(ql_ref.dtype)
    qr_ref[...] = _rope128(qp_ref[...], c_ref[...], s1_ref[...], s2_ref[...])


def _sample_q(q_all, tabs, row0, DB, H, w_ukT):
    KV = w_ukT.shape[2]
    assert row0 % DB == 0
    rb = row0 // DB
    trow = lambda h: (rb, 0)
    return pl.pallas_call(
        _sample_q_kernel,
        grid=(H,),
        in_specs=[
            pl.BlockSpec((DB, LANE), lambda h: (rb, h)),
            pl.BlockSpec((DB, LANE), lambda h: (rb, H + h)),
            pl.BlockSpec((DB, LANE), trow),
            pl.BlockSpec((DB, LANE), trow),
            pl.BlockSpec((DB, LANE), trow),
            pl.BlockSpec((1, LANE, KV), lambda h: (h, 0, 0)),
        ],
        out_specs=[pl.BlockSpec((DB, KV), lambda h: (0, h)), pl.BlockSpec((DB, LANE), lambda h: (0, h))],
        out_shape=[jax.ShapeDtypeStruct((DB, H * KV), BF16), jax.ShapeDtypeStruct((DB, H * LANE), F32)],
        compiler_params=_params("parallel"),
        name="sample_q",
    )(q_all, q_all, *tabs, w_ukT)


PAGES_PER_STEP = 32


def _attn_sample_kernel(pt_ref, ql_ref, qp_ref, ln_ref, kn_ref, lat_hbm, kr_hbm, o_ref,
                        latbuf_ref, krbuf_ref, m_ref, l_ref, acc_ref, sem, *, G, R, layer, scale):
    b = pl.program_id(0)
    j = pl.program_id(1)
    nj = pl.num_programs(1)
    step = b * nj + j
    slot = step % 2
    PS = latbuf_ref.shape[2]

    def page_copies(bb, jj, sl, p):
        page = pt_ref[bb, jj * G + p]
        return (pltpu.make_async_copy(lat_hbm.at[layer, page], latbuf_ref.at[sl, p], sem.at[sl]),
                pltpu.make_async_copy(kr_hbm.at[layer, page], krbuf_ref.at[sl, p], sem.at[sl]))

    def fetch(bb, jj, sl):
        for p in range(G):
            for c in page_copies(bb, jj, sl, p):
                c.start()

    @pl.when(step == 0)
    def _():
        fetch(0, 0, 0)

    @pl.when(step + 1 < pl.num_programs(0) * nj)
    def _():
        last = j == nj - 1
        fetch(jnp.where(last, b + 1, b), jnp.where(last, 0, j + 1), 1 - slot)

    for p in range(G):
        for c in page_copies(b, j, slot, p):
            c.wait()

    ql = ql_ref[0]
    qp = qp_ref[0][:, :R].astype(BF16)

    @pl.when(j == 0)
    def _():
        lat_new = ln_ref[0].astype(BF16).astype(F32)
        kpe_new = kn_ref[0][:, :R].astype(BF16).astype(F32)
        s0 = (jnp.sum(ql.astype(F32) * lat_new, axis=-1, keepdims=True)
              + jnp.sum(qp.astype(F32) * kpe_new, axis=-1, keepdims=True)) * scale
        m_ref[...] = s0
        l_ref[...] = jnp.ones_like(s0)
        acc_ref[...] = jnp.broadcast_to(lat_new, acc_ref.shape)

    nt = (((1,), (1,)), ((), ()))
    lats = [latbuf_ref[slot, p].astype(BF16) for p in range(G)]
    ss = []
    for p in range(G):
        kr = krbuf_ref[slot, p].astype(BF16)
        ss.append(lax.dot_general(ql, lats[p], nt, preferred_element_type=F32)
                  + jnp.dot(qp, kr, preferred_element_type=F32))
    s = jnp.concatenate(ss, axis=1) * scale
    m = m_ref[...]
    m_new = jnp.maximum(m, jnp.max(s, axis=-1, keepdims=True))
    pr = jnp.exp(s - m_new)
    alpha = jnp.exp(m - m_new)
    l_ref[...] = alpha * l_ref[...] + jnp.sum(pr, axis=-1, keepdims=True)
    pb = pr.astype(BF16)
    pv = jnp.dot(pb[:, 0:PS], lats[0], preferred_element_type=F32)
    for p in range(1, G):
        pv = pv + jnp.dot(pb[:, p * PS:(p + 1) * PS], lats[p], preferred_element_type=F32)
    acc_ref[...] = alpha * acc_ref[...] + pv
    m_ref[...] = m_new

    @pl.when(j == pl.num_programs(1) - 1)
    def _():
        o_ref[0] = acc_ref[...] / l_ref[...]


def _attn_sample(page_table, qlat, qrope, lat_new, kpe_new, cache_latent, cache_kropeT, layer, scale):
    DB, H, KV = qlat.shape
    n_pages = page_table.shape[1]
    PS = cache_latent.shape[2]
    R = cache_kropeT.shape[2]
    G = _tile(n_pages, PAGES_PER_STEP, 1)

    per_b = lambda shape: pl.BlockSpec((1,) + shape, lambda b, j, pt: (b, 0, 0))
    grid_spec = pltpu.PrefetchScalarGridSpec(
        num_scalar_prefetch=1,
        grid=(DB, n_pages // G),
        in_specs=[per_b((H, KV)), per_b((H, LANE)), per_b((1, KV)), per_b((1, LANE)),
                  pl.BlockSpec(memory_space=pl.ANY), pl.BlockSpec(memory_space=pl.ANY)],
        out_specs=per_b((H, KV)),
        scratch_shapes=[pltpu.VMEM((2, G, PS, KV), F32), pltpu.VMEM((2, G, R, PS), F32),
                        pltpu.VMEM((H, 1), F32), pltpu.VMEM((H, 1), F32), pltpu.VMEM((H, KV), F32),
                        pltpu.SemaphoreType.DMA((2,))],
    )
    return pl.pallas_call(
        functools.partial(_attn_sample_kernel, G=G, R=R, layer=layer, scale=scale),
        grid_spec=grid_spec,
        out_shape=jax.ShapeDtypeStruct((DB, H, KV), F32),
        compiler_params=_params("arbitrary", "arbitrary"),
        name="attn_sample",
    )(page_table, qlat, qrope, lat_new, kpe_new, cache_latent, cache_kropeT)


def _sample_out_kernel(o_ref, w_ref, y_ref):
    y_ref[...] = jnp.dot(o_ref[...].astype(BF16), w_ref[0], preferred_element_type=F32).astype(y_ref.dtype)


def _sample_out(o_lat2d, w_uv3):
    DB = o_lat2d.shape[0]
    H, KV, DV = w_uv3.shape
    return pl.pallas_call(
        _sample_out_kernel,
        grid=(H,),
        in_specs=[pl.BlockSpec((DB, KV), lambda h: (0, h)), pl.BlockSpec((1, KV, DV), lambda h: (h, 0, 0))],
        out_specs=pl.BlockSpec((DB, DV), lambda h: (0, h)),
        out_shape=jax.ShapeDtypeStruct((DB, H * DV), BF16),
        compiler_params=_params("parallel"),
        name="sample_out",
    )(o_lat2d, w_uv3)


ACC_COLS = 512


def _accumulate_dot(o_ref, a, w_ref, first):
    N = o_ref.shape[1]
    cw = _tile(N, ACC_COLS, LANE)

    @pl.when(first)
    def _():
        o_ref[...] = jnp.zeros(o_ref.shape, o_ref.dtype)

    for c in range(N // cw):
        sl = slice(c * cw, (c + 1) * cw)
        o_ref[:, sl] += jnp.dot(a, w_ref[:, sl], preferred_element_type=F32)


def _proj_out_kernel(a_ref, w_ref, g_ref, x_ref, o_ref):
    k = pl.program_id(1)
    _accumulate_dot(o_ref, a_ref[...], w_ref, k == 0)

    @pl.when(k == pl.num_programs(1) - 1)
    def _():
        _add_rms_inplace(o_ref, x_ref, g_ref)


def _proj_out(a, w, g, x):
    M, K = a.shape
    N = w.shape[1]
    tm = _tile(M, 640, 16)
    tk = _tile(K, 1024, LANE)
    return pl.pallas_call(
        _proj_out_kernel,
        grid=(M // tm, K // tk),
        in_specs=[
            pl.BlockSpec((tm, tk), lambda i, k: (i, k)),
            pl.BlockSpec((tk, N), lambda i, k: (k, 0)),
            pl.BlockSpec((1, N), lambda i, k: (0, 0)),
            pl.BlockSpec((tm, N), lambda i, k: (i, 0), pipeline_mode=pl.Buffered(1)),
        ],
        out_specs=pl.BlockSpec((tm, N), lambda i, k: (i, 0)),
        out_shape=jax.ShapeDtypeStruct((M, N), F32),
        compiler_params=_params("parallel", "arbitrary"),
        name="proj_out",
    )(a, w, g, x)


def _swiglu_hidden(xn, wg, wu):
    g = jnp.dot(xn, wg, preferred_element_type=F32)
    u = jnp.dot(xn, wu, preferred_element_type=F32)
    return (g * jax.nn.sigmoid(g) * u).astype(BF16)


def _ffn_kernel(x_ref, gpre_ref, wg_ref, wu_ref, wd_ref, gpost_ref, o_ref, xn_ref):
    f = pl.program_id(1)

    @pl.when(f == 0)
    def _():
        _rms_to_bf16(x_ref, gpre_ref, xn_ref)

    h = _swiglu_hidden(xn_ref[...], wg_ref[...], wu_ref[...])
    _accumulate_dot(o_ref, h, wd_ref, f == 0)

    @pl.when(f == pl.num_programs(1) - 1)
    def _():
        _add_rms_inplace(o_ref, x_ref, gpost_ref)


def _ffn(x, g_pre, wg, wu, wd, g_post):
    M, D = x.shape
    F = wg.shape[1]
    tm = _tile(M, 640, 16)
    tf = _tile(F, 256, LANE)
    return pl.pallas_call(
        _ffn_kernel,
        grid=(M // tm, F // tf),
        in_specs=[
            pl.BlockSpec((tm, D), lambda i, f: (i, 0), pipeline_mode=pl.Buffered(1)),
            pl.BlockSpec((1, D), lambda i, f: (0, 0)),
            pl.BlockSpec((D, tf), lambda i, f: (0, f)),
            pl.BlockSpec((D, tf), lambda i, f: (0, f)),
            pl.BlockSpec((tf, D), lambda i, f: (f, 0)),
            pl.BlockSpec((1, D), lambda i, f: (0, 0)),
        ],
        out_specs=pl.BlockSpec((tm, D), lambda i, f: (i, 0)),
        out_shape=jax.ShapeDtypeStruct((M, D), F32),
        scratch_shapes=[pltpu.VMEM((tm, D), BF16)],
        compiler_params=_params("parallel", "arbitrary"),
        name="ffn",
    )(x, g_pre, wg, wu, wd, g_post)


def _router_kernel(x_ref, g_ref, w_ref, ti_ref, tg_ref, *, E):
    xn = _rms(x_ref[...], g_ref[...])
    logits = jnp.dot(xn, w_ref[...], preferred_element_type=F32, precision=lax.Precision.HIGHEST)
    lane = lax.broadcasted_iota(jnp.int32, logits.shape, 1)
    logits = jnp.where(lane < E, logits, -jnp.inf)
    mx = jnp.max(logits, axis=-1, keepdims=True)
    ex = jnp.exp(logits - mx)
    probs = ex / jnp.sum(ex, axis=-1, keepdims=True)
    p1 = jnp.max(probs, axis=-1, keepdims=True)
    i1 = jnp.min(jnp.where(probs == p1, lane, LANE), axis=-1, keepdims=True)
    rest = jnp.where(lane == i1, -1.0, probs)
    p2 = jnp.max(rest, axis=-1, keepdims=True)
    i2 = jnp.min(jnp.where(rest == p2, lane, LANE), axis=-1, keepdims=True)
    den = p1 + p2
    ti_ref[...] = jnp.where(lane == 0, i1, jnp.where(lane == 1, i2, 0))
    tg_ref[...] = jnp.where(lane == 0, p1 / den, jnp.where(lane == 1, p2 / den, 0.0))


def _router(x, g, w_router_pad, E):
    M, D = x.shape
    tm = _tile(M, 320, 16)
    return pl.pallas_call(
        functools.partial(_router_kernel, E=E),
        grid=(M // tm,),
        in_specs=[
            pl.BlockSpec((tm, D), lambda i: (i, 0)),
            pl.BlockSpec((1, D), lambda i: (0, 0)),
            pl.BlockSpec((D, LANE), lambda i: (0, 0)),
        ],
        out_specs=[pl.BlockSpec((tm, LANE), lambda i: (i, 0)), pl.BlockSpec((tm, LANE), lambda i: (i, 0))],
        out_shape=[jax.ShapeDtypeStruct((M, LANE), jnp.int32), jax.ShapeDtypeStruct((M, LANE), F32)],
        compiler_params=_params("parallel"),
        name="router",
    )(x, g, w_router_pad)


def _row_gather(src_hbm, dst_ref, sem, n, index_of):
    def issue(r, c):
        pltpu.make_async_copy(src_hbm.at[pl.ds(index_of(r), 1)], dst_ref.at[pl.ds(r, 1)], sem).start()
        return c

    lax.fori_loop(0, n, issue, 0)

    def drain(r, c):
        pltpu.make_async_copy(src_hbm.at[pl.ds(0, 1)], dst_ref.at[pl.ds(r, 1)], sem).wait()
        return c

    lax.fori_loop(0, n, drain, 0)


def _experts_kernel(te_ref, nt_ref, tok_ref, x_hbm, gpre_ref, wg_ref, wu_ref, wd_ref, o_ref,
                    xbuf_ref, xn_ref, sem, *, tm):
    i = pl.program_id(0)
    f = pl.program_id(1)

    @pl.when(i < nt_ref[0])
    def _():
        @pl.when(f == 0)
        def _():
            _row_gather(x_hbm, xbuf_ref, sem, tm, lambda r: tok_ref[i * tm + r])
            _rms_to_bf16(xbuf_ref, gpre_ref, xn_ref)

        h = _swiglu_hidden(xn_ref[...], wg_ref[0], wu_ref[0])
        _accumulate_dot(o_ref, h, wd_ref.at[0], f == 0)

    @pl.when((i >= nt_ref[0]) & (f == 0))
    def _():
        o_ref[...] = jnp.zeros(o_ref.shape, o_ref.dtype)


def _experts(tile_expert, n_tiles, row_token, x, g_pre, wg, wu, wd, tm, n_tiles_max):
    D = x.shape[1]
    F = wg.shape[2]
    tf = _tile(F, 256, LANE)
    nf = F // tf

    def live(i, nt):
        return jnp.minimum(i, nt[0] - 1)

    def fidx(i, f, nt):
        return jnp.where(i < nt[0], f, nf - 1)

    grid_spec = pltpu.PrefetchScalarGridSpec(
        num_scalar_prefetch=3,
        grid=(n_tiles_max, nf),
        in_specs=[
            pl.BlockSpec(memory_space=pl.ANY),
            pl.BlockSpec((1, D), lambda i, f, te, nt, tok: (0, 0)),
            pl.BlockSpec((1, D, tf), lambda i, f, te, nt, tok: (te[live(i, nt)], 0, fidx(i, f, nt))),
            pl.BlockSpec((1, D, tf), lambda i, f, te, nt, tok: (te[live(i, nt)], 0, fidx(i, f, nt))),
            pl.BlockSpec((1, tf, D), lambda i, f, te, nt, tok: (te[live(i, nt)], fidx(i, f, nt), 0)),
        ],
        out_specs=pl.BlockSpec((tm, D), lambda i, f, te, nt, tok: (i, 0)),
        scratch_shapes=[pltpu.VMEM((tm, D), F32), pltpu.VMEM((tm, D), BF16), pltpu.SemaphoreType.DMA(())],
    )
    return pl.pallas_call(
        functools.partial(_experts_kernel, tm=tm),
        grid_spec=grid_spec,
        out_shape=jax.ShapeDtypeStruct((n_tiles_max * tm, D), F32),
        compiler_params=_params("arbitrary", "arbitrary"),
        name="experts",
    )(tile_expert, n_tiles, row_token, x, g_pre, wg, wu, wd)


def _combine_kernel(pos_ref, y_hbm, x_ref, tg_ref, g_ref, o_ref, ybuf_ref, sem, *, tc):
    i = pl.program_id(0)
    _row_gather(y_hbm, ybuf_ref, sem, 2 * tc, lambda r: pos_ref[i * 2 * tc + r])
    tg = tg_ref[...]
    mixed = tg[:, 0:1] * ybuf_ref[0:tc, :] + tg[:, 1:2] * ybuf_ref[tc:2 * tc, :]
    o_ref[...] = x_ref[...] + _rms(mixed, g_ref[...])


def _combine(pos_tiles, y, x, top_g, g_post, tc):
    M, D = x.shape
    grid_spec = pltpu.PrefetchScalarGridSpec(
        num_scalar_prefetch=1,
        grid=(M // tc,),
        in_specs=[
            pl.BlockSpec(memory_space=pl.ANY),
            pl.BlockSpec((tc, D), lambda i, pos: (i, 0)),
            pl.BlockSpec((tc, LANE), lambda i, pos: (i, 0)),
            pl.BlockSpec((1, D), lambda i, pos: (0, 0)),
        ],
        out_specs=pl.BlockSpec((tc, D), lambda i, pos: (i, 0)),
        scratch_shapes=[pltpu.VMEM((2 * tc, D), F32), pltpu.SemaphoreType.DMA(())],
    )
    return pl.pallas_call(
        functools.partial(_combine_kernel, tc=tc),
        grid_spec=grid_spec,
        out_shape=jax.ShapeDtypeStruct((M, D), F32),
        compiler_params=_params("arbitrary"),
        name="combine",
    )(pos_tiles, y, x, top_g, g_post)


def _moe(x, g_pre, w_router, wg, wu, wd, g_post):
    M, D = x.shape
    E = w_router.shape[1]
    wr = jnp.zeros((D, LANE), F32).at[:, :E].set(w_router)
    top_i, top_g = _router(x, g_pre, wr, E)
    tm = _tile(M, 512, 16) if M < 512 else 512
    e_flat = top_i[:, :2].reshape(-1)
    onehot = (e_flat[:, None] == jnp.arange(E, dtype=jnp.int32)[None, :]).astype(jnp.int32)
    counts = jnp.sum(onehot, axis=0)
    rank = jnp.sum((jnp.cumsum(onehot, axis=0) - onehot) * onehot, axis=1)
    padded = ((counts + tm - 1) // tm) * tm
    ends = jnp.cumsum(padded)
    pos = (ends - padded)[e_flat] + rank
    n_tiles_max = -(-2 * M // tm) + E
    row_token = jnp.zeros((n_tiles_max * tm,), jnp.int32).at[pos].set(jnp.arange(2 * M, dtype=jnp.int32) // 2)
    n_tiles = (ends[-1] // tm).astype(jnp.int32).reshape(1)
    tile_start = jnp.arange(n_tiles_max, dtype=jnp.int32) * tm
    tile_expert = jnp.minimum(jnp.searchsorted(ends, tile_start, side="right"), E - 1).astype(jnp.int32)
    y = _experts(tile_expert, n_tiles, row_token, x, g_pre, wg, wu, wd, tm, n_tiles_max)
    tc = _tile(M, 128, SUBLANE)
    pos_tiles = pos.reshape(M // tc, tc, 2).transpose(0, 2, 1).reshape(-1).astype(jnp.int32)
    return _combine(pos_tiles, y, x, top_g, g_post, tc)


def _ple_kernel(x_ref, xc_ref, g_ref, wg_ref, p_ref, wp_ref, o_ref, xn_ref):
    @pl.when(pl.program_id(1) == 0)
    def _():
        _rms_to_bf16(x_ref, g_ref, xn_ref)

    gate = jax.nn.sigmoid(jnp.dot(xn_ref[...], wg_ref[...], preferred_element_type=F32))
    emb = jnp.dot(p_ref[...].astype(BF16), wp_ref[...], preferred_element_type=F32)
    o_ref[...] = xc_ref[...] + gate * emb


def _ple(x, g, w_gate, p, w_ple):
    M, D = x.shape
    P = p.shape[1]
    tm = _tile(M, 640, 16)
    tn = _tile(D, 1024, LANE)
    return pl.pallas_call(
        _ple_kernel,
        grid=(M // tm, D // tn),
        in_specs=[
            pl.BlockSpec((tm, D), lambda i, j: (i, 0), pipeline_mode=pl.Buffered(1)),
            pl.BlockSpec((tm, tn), lambda i, j: (i, j)),
            pl.BlockSpec((1, D), lambda i, j: (0, 0)),
            pl.BlockSpec((D, tn), lambda i, j: (0, j)),
            pl.BlockSpec((tm, P), lambda i, j: (i, 0)),
            pl.BlockSpec((P, tn), lambda i, j: (0, j)),
        ],
        out_specs=pl.BlockSpec((tm, tn), lambda i, j: (i, j)),
        out_shape=jax.ShapeDtypeStruct((M, D), F32),
        scratch_shapes=[pltpu.VMEM((tm, D), BF16)],
        compiler_params=_params("parallel", "arbitrary"),
        name="ple",
    )(x, x, g, w_gate, p, w_ple)


def _rope_tables(pos):
    half = 32
    inv = ROPE_BASE ** (-jnp.arange(0, 2 * half, 2, dtype=F32) / (2 * half))
    ang = pos.astype(F32)[:, None] * inv[None, :]
    cos, sin = jnp.cos(ang), jnp.sin(ang)
    z = jnp.zeros_like(cos)
    return (jnp.concatenate([cos, cos, z, z], axis=1),
            jnp.concatenate([-sin, z, z, z], axis=1),
            jnp.concatenate([z, sin, z, z], axis=1))


def kernel(x_prompt, x_sample, p_prompt, p_sample, state_conv_a, state_conv_b, state_pool, cache_latent, cache_krope, page_table, g_pre_mix, w_in, conv_a_w, conv_a_b, ln_a_g, ln_a_b, conv_b_w, pool_w, pool_scale, g_q, w_uq, g_kv, w_uk, w_uv, w_out, g_post_mix, g_pre_ffn, g_post_ffn, w_ffn_gate, w_ffn_up, w_ffn_down, w_router, w_exp_gate, w_exp_up, w_exp_down, g_ple, w_ple_gate, w_ple):
    B, S, D = x_prompt.shape
    DB, T, _ = x_sample.shape
    assert T == 1
    depth = w_in.shape[0]
    Np = B * S
    M = Np + DB * T
    W_A = conv_a_w.shape[2]
    W_B = conv_b_w.shape[2]
    W_C = pool_scale.shape[1]
    assert W_A == W_B == W_C
    Q_RANK = g_q.shape[1]
    KV = g_kv.shape[1]
    H = w_uq.shape[2]
    NOPE = w_uk.shape[3]
    ROPE = cache_krope.shape[3]
    DV = w_uv.shape[3]
    assert NOPE == LANE and DV == LANE and ROPE == 64 and w_uq.shape[3] == NOPE + ROPE
    NB = state_pool.shape[2]
    n_past = page_table.shape[1] * cache_latent.shape[2]
    scale = float(NOPE + ROPE) ** -0.5
    N_main = 2 * W_A + 3 * W_B + W_C + Q_RANK + KV
    assert w_in.shape[2] == N_main + ROPE
    col_a, col_b, col_c = 0, 2 * W_A, 2 * W_A + 3 * W_B
    col_q = col_c + W_C
    col_kv = col_q + Q_RANK
    assert col_q % Q_RANK == 0 and col_kv % KV == 0

    x = jnp.concatenate([x_prompt.reshape(Np, D), x_sample.reshape(DB, D)], axis=0)
    pos = jnp.concatenate([jnp.tile(jnp.arange(S), B), jnp.full((DB,), n_past)])
    tabs = _rope_tables(pos)
    cache_kropeT = jnp.transpose(cache_krope, (0, 1, 3, 2))
    row2 = lambda a: a.reshape(1, -1)
    outs = [[] for _ in range(10)]

    for l in range(depth):
        w_main = w_in[l, :, :N_main].astype(BF16)
        w_tail = jnp.pad(w_in[l, :, N_main:], ((0, 0), (0, LANE - ROPE))).astype(BF16)
        wq = w_uq[l]
        wq_r = jnp.concatenate(
            [wq[:, :, :NOPE].reshape(Q_RANK, H * NOPE),
             jnp.pad(wq[:, :, NOPE:], ((0, 0), (0, 0), (0, LANE - ROPE))).reshape(Q_RANK, H * LANE)],
            axis=1).astype(BF16)
        w_kv = jnp.concatenate([w_uk[l].reshape(KV, H * NOPE), w_uv[l].reshape(KV, H * DV)], axis=1).astype(BF16)
        w_ukT = jnp.transpose(w_uk[l], (1, 2, 0)).astype(BF16)
        w_uv3 = jnp.transpose(w_uv[l], (1, 0, 2)).astype(BF16)

        u, kpe_raw = _proj_in(x, row2(g_pre_mix[l]), w_main, w_tail)
        latent, kpe = _latent(u, col_kv // KV, KV, row2(g_kv[l]), kpe_raw, tabs)
        q_all = _matmul(u, wq_r, row2(g_q[l]), x_col=col_q // Q_RANK)

        ya_pre, na_p = _mixa_prompt(u, B, S, W_A, conv_a_w[l], row2(conv_a_b[l]))
        ya_p = _ln_silu(ya_pre, row2(ln_a_g[l]), row2(ln_a_b[l]))
        yb_p, nb_p = _mixb_prompt(u, col_b, B, S, W_B, conv_b_w[l])
        pw = pool_w[l].astype(BF16)
        yc_p, nc_p = _mixc_prompt(u, col_c, B, S, W_C, pw, row2(pool_scale[l]), NB)
        kv = _matmul(latent, w_kv, rows=Np, out_dtype=BF16)
        yd_p = _attn_prompt(q_all, tabs, kv, kpe.astype(BF16), B, S, H, scale)

        sa = jnp.transpose(state_conv_a[l], (1, 0, 2))
        sb = jnp.transpose(state_conv_b[l], (1, 0, 2))
        sc = jnp.transpose(state_pool[l], (1, 0, 2))
        cols = (col_a, col_a + W_A, col_b, col_b + W_B, col_b + 2 * W_B, col_c)
        ya_s, yb_s, yc_s, na_s, nb_s, nc_s = _mix_sample(
            u, Np, cols, DB, W_A, sa, sb, sc, conv_a_w[l], row2(conv_a_b[l]), row2(ln_a_g[l]),
            row2(ln_a_b[l]), conv_b_w[l], pw, row2(pool_scale[l]), n_past)
        qlat, qrope = _sample_q(q_all, tabs, Np, DB, H, w_ukT)
        o_lat = _attn_sample(page_table, qlat.reshape(DB, H, KV), qrope.reshape(DB, H, LANE),
                             latent[Np:].reshape(DB, 1, KV), kpe[Np:].reshape(DB, 1, LANE),
                             cache_latent, cache_kropeT, l, scale)
        yd_s = _sample_out(o_lat.reshape(DB, H * KV), w_uv3)

        mix = jnp.concatenate([jnp.concatenate([ya_p, yb_p, yc_p, yd_p], axis=1),
                               jnp.concatenate([ya_s, yb_s, yc_s, yd_s], axis=1)], axis=0)
        x = _proj_out(mix, w_out[l].astype(BF16), row2(g_post_mix[l]), x)

        j = l // 2
        if l % 2 == 0:
            x = _ffn(x, row2(g_pre_ffn[l]), w_ffn_gate[j].astype(BF16), w_ffn_up[j].astype(BF16),
                     w_ffn_down[j].astype(BF16), row2(g_post_ffn[l]))
        else:
            x = _moe(x, row2(g_pre_ffn[l]), w_router[j], w_exp_gate[j].astype(BF16),
                     w_exp_up[j].astype(BF16), w_exp_down[j].astype(BF16), row2(g_post_ffn[l]))

        p_l = jnp.concatenate([p_prompt[l].reshape(Np, -1), p_sample[l].reshape(DB, -1)], axis=0)
        x = _ple(x, row2(g_ple[l]), w_ple_gate[l].astype(BF16), p_l, w_ple[l].astype(BF16))

        new = (na_p, jnp.transpose(na_s, (1, 0, 2)), nb_p, jnp.transpose(nb_s, (1, 0, 2)),
               nc_p, jnp.transpose(nc_s, (1, 0, 2)),
               latent[:Np].reshape(B, S, KV), latent[Np:].reshape(DB, T, KV),
               kpe[:Np, :ROPE].reshape(B, S, ROPE), kpe[Np:, :ROPE].reshape(DB, T, ROPE))
        for o, v in zip(outs, new):
            o.append(v)

    return (x[:Np].reshape(B, S, D), x[Np:].reshape(DB, T, D)) + tuple(jnp.stack(o) for o in outs)
```

```python
import functools

import jax
import jax.numpy as jnp
from jax import lax
from jax.experimental import pallas as pl
from jax.experimental.pallas import tpu as pltpu

F32 = jnp.float32
BF16 = jnp.bfloat16
EPS = 1e-6
ROPE_BASE = 10000.0
POOL_WINDOWS = (2, 4, 8, 16)
LANE = 128
SUBLANE = 8
VMEM_LIMIT = 56 * 1024 * 1024


def _tile(n, target, align):
    best = None
    d = align
    while d <= min(n, target):
        if n % d == 0:
            best = d
        d += align
    return best if best is not None else n


def _tile_cast_kernel(w_ref, o_ref, *, nt, tn):
    for j in range(nt):
        o_ref[0, j] = w_ref[0, :, j * tn:(j + 1) * tn].astype(o_ref.dtype)


def _col_tiles(w, tn, n_cols=None):
    L, K, N = w.shape
    n_cols = N if n_cols is None else n_cols
    assert n_cols % tn == 0
    nt = n_cols // tn
    kb = _tile(K, 64, 16)
    return pl.pallas_call(
        functools.partial(_tile_cast_kernel, nt=nt, tn=tn),
        grid=(L, K // kb),
        in_specs=[pl.BlockSpec((1, kb, N), lambda l, k: (l, k, 0))],
        out_specs=pl.BlockSpec((1, nt, kb, tn), lambda l, k: (l, 0, k, 0)),
        out_shape=jax.ShapeDtypeStruct((L, nt, K, tn), BF16),
        compiler_params=pltpu.CompilerParams(dimension_semantics=("parallel", "parallel"),
                                             vmem_limit_bytes=VMEM_LIMIT),
        name="tile_cast",
    )(w)


def _params(*sem):
    return pltpu.CompilerParams(dimension_semantics=sem, vmem_limit_bytes=VMEM_LIMIT)


def _rms(x, g):
    ms = jnp.mean(x * x, axis=-1, keepdims=True)
    return x * lax.rsqrt(ms + EPS) * g


NORM_ROWS = 128


def _row_chunks(n_rows, fn):
    rc = _tile(n_rows, NORM_ROWS, 16)

    def body(r, c):
        fn(pl.ds(pl.multiple_of(r * rc, rc), rc))
        return c

    lax.fori_loop(0, n_rows // rc, body, 0)


def _rms_to_bf16(x_ref, g_ref, xn_ref, norm=True):
    def fn(rows):
        x = x_ref[rows, :].astype(F32)
        xn_ref[rows, :] = (_rms(x, g_ref[...]) if norm else x).astype(BF16)

    _row_chunks(x_ref.shape[0], fn)


def _add_rms_inplace(o_ref, x_ref, g_ref):
    def fn(rows):
        o_ref[rows, :] = x_ref[rows, :] + _rms(o_ref[rows, :], g_ref[...])

    _row_chunks(o_ref.shape[0], fn)


def _rope128(x, c, s1, s2):
    return x * c + pltpu.roll(x, 96, 1) * s1 + pltpu.roll(x, 32, 1) * s2


def _mm_kernel(x_ref, g_ref, w_ref, o_ref, xn_ref, *, norm):
    @pl.when(pl.program_id(1) == 0)
    def _():
        _rms_to_bf16(x_ref, g_ref, xn_ref, norm)

    o_ref[...] = jnp.dot(xn_ref[...], w_ref[...], preferred_element_type=F32).astype(o_ref.dtype)


def _matmul(x, w, g=None, *, rows=None, row0=0, x_col=0, out_dtype=F32, tm_target=640, tn_target=512):
    K, N = w.shape
    rows = x.shape[0] if rows is None else rows
    tm = _tile(rows, tm_target, 16)
    tn = _tile(N, tn_target, LANE)
    assert row0 % tm == 0
    norm = g is not None
    if g is None:
        g = jnp.ones((1, K), F32)
    rb = row0 // tm
    return pl.pallas_call(
        functools.partial(_mm_kernel, norm=norm),
        grid=(rows // tm, N // tn),
        in_specs=[
            pl.BlockSpec((tm, K), lambda i, j: (rb + i, x_col)),
            pl.BlockSpec((1, K), lambda i, j: (0, 0)),
            pl.BlockSpec((K, tn), lambda i, j: (0, j)),
        ],
        out_specs=pl.BlockSpec((tm, tn), lambda i, j: (i, j)),
        out_shape=jax.ShapeDtypeStruct((rows, N), out_dtype),
        scratch_shapes=[pltpu.VMEM((tm, K), BF16)],
        compiler_params=_params("parallel", "arbitrary"),
        name="matmul",
    )(x, g, w)


def _proj_in_kernel(x_ref, g_ref, w_ref, wt_ref, o_ref, ot_ref, xn_ref):
    @pl.when(pl.program_id(1) == 0)
    def _():
        _rms_to_bf16(x_ref, g_ref, xn_ref)
        ot_ref[...] = jnp.dot(xn_ref[...], wt_ref[...], preferred_element_type=F32)

    o_ref[...] = jnp.dot(xn_ref[...], w_ref[0, 0], preferred_element_type=F32)


def _proj_in(x, g, w_tiles, layer, w_tail):
    M, K = x.shape
    _, nt, _, tn = w_tiles.shape
    N = nt * tn
    tm = _tile(M, 640, 16)
    return pl.pallas_call(
        _proj_in_kernel,
        grid=(M // tm, N // tn),
        in_specs=[
            pl.BlockSpec((tm, K), lambda i, j: (i, 0)),
            pl.BlockSpec((1, K), lambda i, j: (0, 0)),
            pl.BlockSpec((1, 1, K, tn), lambda i, j: (layer, j, 0, 0)),
            pl.BlockSpec((K, LANE), lambda i, j: (0, 0)),
        ],
        out_specs=[
            pl.BlockSpec((tm, tn), lambda i, j: (i, j)),
            pl.BlockSpec((tm, LANE), lambda i, j: (i, 0)),
        ],
        out_shape=[
            jax.ShapeDtypeStruct((M, N), F32),
            jax.ShapeDtypeStruct((M, LANE), F32),
        ],
        scratch_shapes=[pltpu.VMEM((tm, K), BF16)],
        compiler_params=_params("parallel", "arbitrary"),
        name="proj_in",
    )(x, g, w_tiles, w_tail)


def _latent_kernel(ckv_ref, g_ref, kr_ref, c_ref, s1_ref, s2_ref, lat_ref, kpe_ref):
    lat_ref[...] = _rms(ckv_ref[...], g_ref[...])
    kpe_ref[...] = _rope128(kr_ref[...], c_ref[...], s1_ref[...], s2_ref[...])


def _latent(u, ckv_col, kv_rank, g_kv, kpe_raw, tabs):
    M = u.shape[0]
    tm = _tile(M, 640, 16)
    row = lambda i: (i, 0)
    return pl.pallas_call(
        _latent_kernel,
        grid=(M // tm,),
        in_specs=[
            pl.BlockSpec((tm, kv_rank), lambda i: (i, ckv_col)),
            pl.BlockSpec((1, kv_rank), lambda i: (0, 0)),
            pl.BlockSpec((tm, LANE), row),
            pl.BlockSpec((tm, LANE), row),
            pl.BlockSpec((tm, LANE), row),
            pl.BlockSpec((tm, LANE), row),
        ],
        out_specs=[pl.BlockSpec((tm, kv_rank), row), pl.BlockSpec((tm, LANE), row)],
        out_shape=[jax.ShapeDtypeStruct((M, kv_rank), F32), jax.ShapeDtypeStruct((M, LANE), F32)],
        compiler_params=_params("parallel"),
        name="latent",
    )(u, g_kv, kpe_raw, *tabs)


CONV_ROWS = 128


def _mixa_prompt_kernel(val_ref, gate_ref, w_ref, b_ref, y_ref, st_ref, hpad_ref, *, S, KW, pad):
    C = val_ref.shape[1]
    h = val_ref[...] * jax.nn.sigmoid(gate_ref[...])
    hpad_ref[0:pad, :] = jnp.zeros((pad, C), F32)
    hpad_ref[pad:pad + S, :] = h
    st_ref[0] = hpad_ref[pad + S - (KW - 1):pad + S, :]
    off = pad - (KW - 1)
    R = min(CONV_ROWS, S)
    for r in range(S // R):
        acc = jnp.broadcast_to(b_ref[...], (R, C))
        for k in range(KW):
            lo = off + r * R + k
            acc = acc + w_ref[k:k + 1, :] * hpad_ref[lo:lo + R, :]
        y_ref[r * R:(r + 1) * R, :] = acc


def _mixa_prompt(u, B, S, W, conv_w, conv_b):
    KW = conv_w.shape[0]
    pad = -(-(KW - 1) // SUBLANE) * SUBLANE
    C = LANE
    nc = W // C
    return pl.pallas_call(
        functools.partial(_mixa_prompt_kernel, S=S, KW=KW, pad=pad),
        grid=(B, nc),
        in_specs=[
            pl.BlockSpec((S, C), lambda b, c: (b, c)),
            pl.BlockSpec((S, C), lambda b, c: (b, nc + c)),
            pl.BlockSpec((KW, C), lambda b, c: (0, c)),
            pl.BlockSpec((1, C), lambda b, c: (0, c)),
        ],
        out_specs=[
            pl.BlockSpec((S, C), lambda b, c: (b, c)),
            pl.BlockSpec((1, KW - 1, C), lambda b, c: (b, 0, c)),
        ],
        out_shape=[
            jax.ShapeDtypeStruct((B * S, W), F32),
            jax.ShapeDtypeStruct((B, KW - 1, W), F32),
        ],
        scratch_shapes=[pltpu.VMEM((pad + S, C), F32)],
        compiler_params=_params("parallel", "parallel"),
        name="mixa_prompt",
    )(u, u, conv_w, conv_b)


def _ln_silu_kernel(x_ref, g_ref, b_ref, o_ref):
    x = x_ref[...]
    mu = jnp.mean(x, axis=-1, keepdims=True)
    xc = x - mu
    var = jnp.mean(xc * xc, axis=-1, keepdims=True)
    y = xc * lax.rsqrt(var + EPS) * g_ref[...] + b_ref[...]
    o_ref[...] = (y * jax.nn.sigmoid(y)).astype(o_ref.dtype)


def _ln_silu(x, g, b):
    M, W = x.shape
    tm = _tile(M, 512, 16)
    return pl.pallas_call(
        _ln_silu_kernel,
        grid=(M // tm,),
        in_specs=[
            pl.BlockSpec((tm, W), lambda i: (i, 0)),
            pl.BlockSpec((1, W), lambda i: (0, 0)),
            pl.BlockSpec((1, W), lambda i: (0, 0)),
        ],
        out_specs=pl.BlockSpec((tm, W), lambda i: (i, 0)),
        out_shape=jax.ShapeDtypeStruct((M, W), BF16),
        compiler_params=_params("parallel"),
        name="ln_silu",
    )(x, g, b)


def _mixb_prompt_kernel(bb_ref, bc_ref, bx_ref, w_ref, y_ref, st_ref, gpad_ref, *, S, KW, pad):
    C = bb_ref.shape[1]
    g = bc_ref[...] * bx_ref[...]
    gpad_ref[0:pad, :] = jnp.zeros((pad, C), F32)
    gpad_ref[pad:pad + S, :] = g
    st_ref[0] = gpad_ref[pad + S - (KW - 1):pad + S, :]
    off = pad - (KW - 1)
    R = min(CONV_ROWS, S)
    for r in range(S // R):
        acc = jnp.zeros((R, C), F32)
        for k in range(KW):
            lo = off + r * R + k
            acc = acc + w_ref[k:k + 1, :] * gpad_ref[lo:lo + R, :]
        y_ref[r * R:(r + 1) * R, :] = (bb_ref[r * R:(r + 1) * R, :] * acc).astype(y_ref.dtype)


def _mixb_prompt(u, col0, B, S, W, conv_w):
    KW = conv_w.shape[0]
    pad = -(-(KW - 1) // SUBLANE) * SUBLANE
    C = LANE
    nc = W // C
    c0 = col0 // C
    return pl.pallas_call(
        functools.partial(_mixb_prompt_kernel, S=S, KW=KW, pad=pad),
        grid=(B, nc),
        in_specs=[
            pl.BlockSpec((S, C), lambda b, c: (b, c0 + c)),
            pl.BlockSpec((S, C), lambda b, c: (b, c0 + nc + c)),
            pl.BlockSpec((S, C), lambda b, c: (b, c0 + 2 * nc + c)),
            pl.BlockSpec((KW, C), lambda b, c: (0, c)),
        ],
        out_specs=[
            pl.BlockSpec((S, C), lambda b, c: (b, c)),
            pl.BlockSpec((1, KW - 1, C), lambda b, c: (b, 0, c)),
        ],
        out_shape=[
            jax.ShapeDtypeStruct((B * S, W), BF16),
            jax.ShapeDtypeStruct((B, KW - 1, W), F32),
        ],
        scratch_shapes=[pltpu.VMEM((pad + S, C), F32)],
        compiler_params=_params("parallel", "parallel"),
        name="mixb_prompt",
    )(u, u, u, conv_w)


POOL_ROWS = 256


def _mixc_prompt_kernel(v_ref, w_ref, sc_ref, y_ref, st_ref, vpad_ref, *, S, NB):
    C = v_ref.shape[1]
    pad = NB + 1
    grp = pl.program_id(1)
    vpad_ref[0:pad, :] = jnp.zeros((pad, C), F32)
    vpad_ref[pad:pad + S, :] = v_ref[...]
    st_ref[0] = vpad_ref[pad + S - NB:pad + S, :]
    w = w_ref[0]
    R = min(POOL_ROWS, S)
    win = jnp.left_shift(2, grp).astype(F32)
    for r in range(S // R):
        base = pad + r * R
        v = vpad_ref[base:base + R, :]
        acc = v
        sel = None
        for j in range(1, POOL_WINDOWS[-1]):
            acc = acc + vpad_ref[base - j:base - j + R, :]
            if j + 1 in POOL_WINDOWS:
                gi = POOL_WINDOWS.index(j + 1)
                sel = acc if sel is None else jnp.where(grp >= gi, acc, sel)
        pos1 = (lax.broadcasted_iota(jnp.int32, (R, 1), 0) + (r * R + 1)).astype(F32)
        d = sel / jnp.minimum(pos1, win) - v
        y = jnp.dot(d.astype(BF16), w, preferred_element_type=F32) * sc_ref[...]
        y_ref[r * R:(r + 1) * R, :] = y.astype(y_ref.dtype)


def _mixc_prompt(u, col0, B, S, W, pool_w, pool_scale, NB):
    G = pool_w.shape[0]
    C = W // G
    c0 = col0 // C
    assert col0 % C == 0 and G == len(POOL_WINDOWS)
    return pl.pallas_call(
        functools.partial(_mixc_prompt_kernel, S=S, NB=NB),
        grid=(B, G),
        in_specs=[
            pl.BlockSpec((S, C), lambda b, c: (b, c0 + c)),
            pl.BlockSpec((1, C, C), lambda b, c: (c, 0, 0)),
            pl.BlockSpec((1, C), lambda b, c: (0, c)),
        ],
        out_specs=[
            pl.BlockSpec((S, C), lambda b, c: (b, c)),
            pl.BlockSpec((1, NB, C), lambda b, c: (b, 0, c)),
        ],
        out_shape=[
            jax.ShapeDtypeStruct((B * S, W), BF16),
            jax.ShapeDtypeStruct((B, NB, W), F32),
        ],
        scratch_shapes=[pltpu.VMEM((NB + 1 + S, C), F32)],
        compiler_params=_params("parallel", "arbitrary"),
        name="mixc_prompt",
    )(u, pool_w, pool_scale)


def _mix_sample_kernel(av_ref, ag_ref, bb_ref, bc_ref, bx_ref, cv_ref,
                       sa_ref, sb_ref, sc_ref,
                       wa_ref, ba_ref, lg_ref, lb_ref, wb_ref, wp_ref, ps_ref,
                       ya_ref, yb_ref, yc_ref, na_ref, nb_ref, nc_ref, *, cnt):
    KA = wa_ref.shape[0]
    KB = wb_ref.shape[0]
    NB = sc_ref.shape[0]
    h = av_ref[...] * jax.nn.sigmoid(ag_ref[...])
    acc = ba_ref[...] + wa_ref[KA - 1:KA, :] * h
    for k in range(KA - 1):
        acc = acc + wa_ref[k:k + 1, :] * sa_ref[k]
    mu = jnp.mean(acc, axis=-1, keepdims=True)
    xc = acc - mu
    var = jnp.mean(xc * xc, axis=-1, keepdims=True)
    y = xc * lax.rsqrt(var + EPS) * lg_ref[...] + lb_ref[...]
    ya_ref[...] = (y * jax.nn.sigmoid(y)).astype(ya_ref.dtype)
    for k in range(KA - 2):
        na_ref[k] = sa_ref[k + 1]
    na_ref[KA - 2] = h
    g = bc_ref[...] * bx_ref[...]
    accb = wb_ref[KB - 1:KB, :] * g
    for k in range(KB - 1):
        accb = accb + wb_ref[k:k + 1, :] * sb_ref[k]
    yb_ref[...] = (bb_ref[...] * accb).astype(yb_ref.dtype)
    for k in range(KB - 2):
        nb_ref[k] = sb_ref[k + 1]
    nb_ref[KB - 2] = g
    v = cv_ref[...]
    G = wp_ref.shape[0]
    C = wp_ref.shape[1]
    for gi in range(G):
        sl = slice(gi * C, (gi + 1) * C)
        vg = v[:, sl]
        t = vg
        for j in range(1, POOL_WINDOWS[gi]):
            t = t + sc_ref[NB - j][:, sl]
        d = t / cnt[gi] - vg
        yg = jnp.dot(d.astype(BF16), wp_ref[gi], preferred_element_type=F32) * ps_ref[:, sl]
        yc_ref[:, sl] = yg.astype(yc_ref.dtype)
    for k in range(NB - 1):
        nc_ref[k] = sc_ref[k + 1]
    nc_ref[NB - 1] = v


def _mix_sample(u, row0, cols, DB, W, sa, sb, sc, wa, ba, lg, lb, wb, wp, ps, n_past):
    bt = _tile(DB, 16, SUBLANE)
    assert row0 % bt == 0 and all(c % W == 0 for c in cols)
    rb = row0 // bt
    KA, KB, NB = wa.shape[0], wb.shape[0], sc.shape[0]
    cnt = tuple(float(min(n_past + 1, w)) for w in POOL_WINDOWS)
    useg = [pl.BlockSpec((bt, W), functools.partial(lambda i, c: (rb + i, c), c=c // W)) for c in cols]
    st = lambda n: pl.BlockSpec((n, bt, W), lambda i: (0, i, 0))
    full = lambda a: pl.BlockSpec(a.shape, lambda i: (0,) * a.ndim)
    row = pl.BlockSpec((bt, W), lambda i: (i, 0))
    return pl.pallas_call(
        functools.partial(_mix_sample_kernel, cnt=cnt),
        grid=(DB // bt,),
        in_specs=useg + [st(KA - 1), st(KB - 1), st(NB)] + [full(a) for a in (wa, ba, lg, lb, wb, wp, ps)],
        out_specs=[row, row, row, st(KA - 1), st(KB - 1), st(NB)],
        out_shape=[
            jax.ShapeDtypeStruct((DB, W), BF16),
            jax.ShapeDtypeStruct((DB, W), BF16),
            jax.ShapeDtypeStruct((DB, W), BF16),
            jax.ShapeDtypeStruct(sa.shape, F32),
            jax.ShapeDtypeStruct(sb.shape, F32),
            jax.ShapeDtypeStruct(sc.shape, F32),
        ],
        compiler_params=_params("parallel"),
        name="mix_sample",
    )(u, u, u, u, u, u, sa, sb, sc, wa, ba, lg, lb, wb, wp, ps)


ATTN_TILE = 512
HEADS_PER_STEP = 2


def _attn_prompt_kernel(qn_ref, qp_ref, c_ref, s1_ref, s2_ref, kn_ref, kp_ref, v_ref, o_ref,
                        q_scr, m_scr, l_scr, acc_scr, *, t, HB, scale):
    qi = pl.program_id(2)
    c, s1, s2 = c_ref[...], s1_ref[...], s2_ref[...]
    for hh in range(HB):
        hs = slice(hh * LANE, (hh + 1) * LANE)
        qp = _rope128(qp_ref[:, hs], c, s1, s2)
        q_scr[hh] = jnp.concatenate([qn_ref[:, hs], qp], axis=1).astype(BF16)
        m_scr[hh] = jnp.full((t, 1), -jnp.inf, F32)
        l_scr[hh] = jnp.zeros((t, 1), F32)
        acc_scr[hh] = jnp.zeros((t, LANE), F32)

    def tile(j, masked):
        ks = pl.multiple_of(j * t, t)
        kp = kp_ref[pl.ds(ks, t), :]
        for hh in range(HB):
            hs = slice(hh * LANE, (hh + 1) * LANE)
            k = jnp.concatenate([kn_ref[pl.ds(ks, t), hs], kp], axis=1)
            s = lax.dot_general(q_scr[hh], k, (((1,), (1,)), ((), ())), preferred_element_type=F32) * scale
            if masked:
                row = lax.broadcasted_iota(jnp.int32, (t, t), 0)
                col = lax.broadcasted_iota(jnp.int32, (t, t), 1)
                s = jnp.where(col <= row, s, -jnp.inf)
            m = m_scr[hh]
            m_new = jnp.maximum(m, jnp.max(s, axis=-1, keepdims=True))
            p = jnp.exp(s - m_new)
            alpha = jnp.exp(m - m_new)
            l_scr[hh] = alpha * l_scr[hh] + jnp.sum(p, axis=-1, keepdims=True)
            acc_scr[hh] = alpha * acc_scr[hh] + jnp.dot(p.astype(BF16), v_ref[pl.ds(ks, t), hs],
                                                        preferred_element_type=F32)
            m_scr[hh] = m_new

    def body(j, carry):
        tile(j, False)
        return carry

    lax.fori_loop(0, qi, body, 0)
    tile(qi, True)
    for hh in range(HB):
        o_ref[:, hh * LANE:(hh + 1) * LANE] = (acc_scr[hh] / l_scr[hh]).astype(o_ref.dtype)


def _attn_prompt(q_all, tabs, kv, kpe_bf, B, S, H, scale):
    t = _tile(S, ATTN_TILE, 16)
    nq = S // t
    HB = HEADS_PER_STEP if H % HEADS_PER_STEP == 0 else 1
    HG = H // HB
    W = HB * LANE
    qrow = lambda b, h, i: (b * nq + i, 0)
    return pl.pallas_call(
        functools.partial(_attn_prompt_kernel, t=t, HB=HB, scale=scale),
        grid=(B, HG, nq),
        in_specs=[
            pl.BlockSpec((t, W), lambda b, h, i: (b * nq + i, h)),
            pl.BlockSpec((t, W), lambda b, h, i: (b * nq + i, HG + h)),
            pl.BlockSpec((t, LANE), qrow),
            pl.BlockSpec((t, LANE), qrow),
            pl.BlockSpec((t, LANE), qrow),
            pl.BlockSpec((S, W), lambda b, h, i: (b, h)),
            pl.BlockSpec((S, LANE), lambda b, h, i: (b, 0)),
            pl.BlockSpec((S, W), lambda b, h, i: (b, HG + h)),
        ],
        out_specs=pl.BlockSpec((t, W), lambda b, h, i: (b * nq + i, h)),
        out_shape=jax.ShapeDtypeStruct((B * S, H * LANE), BF16),
        scratch_shapes=[pltpu.VMEM((HB, t, 2 * LANE), BF16), pltpu.VMEM((HB, t, 1), F32),
                        pltpu.VMEM((HB, t, 1), F32), pltpu.VMEM((HB, t, LANE), F32)],
        compiler_params=_params("parallel", "parallel", "arbitrary"),
        name="attn_prompt",
    )(q_all, q_all, *tabs, kv, kpe_bf, kv)


def _sample_q_kernel(qn_ref, qp_ref, c_ref, s1_ref, s2_ref, wk_ref, ql_ref, qr_ref):
    ql_ref[...] = jnp.dot(qn_ref[...].astype(BF16), wk_ref[0], preferred_element_type=F32).astype(ql_ref.dtype)
    qr_ref[...] = _rope128(qp_ref[...], c_ref[...], s1_ref[...], s2_ref[...])


def _sample_q(q_all, tabs, row0, DB, H, w_ukT):
    KV = w_ukT.shape[2]
    assert row0 % DB == 0
    rb = row0 // DB
    trow = lambda h: (rb, 0)
    return pl.pallas_call(
        _sample_q_kernel,
        grid=(H,),
        in_specs=[
            pl.BlockSpec((DB, LANE), lambda h: (rb, h)),
            pl.BlockSpec((DB, LANE), lambda h: (rb, H + h)),
            pl.BlockSpec((DB, LANE), trow),
            pl.BlockSpec((DB, LANE), trow),
            pl.BlockSpec((DB, LANE), trow),
            pl.BlockSpec((1, LANE, KV), lambda h: (h, 0, 0)),
        ],
        out_specs=[pl.BlockSpec((DB, KV), lambda h: (0, h)), pl.BlockSpec((DB, LANE), lambda h: (0, h))],
        out_shape=[jax.ShapeDtypeStruct((DB, H * KV), BF16), jax.ShapeDtypeStruct((DB, H * LANE), F32)],
        compiler_params=_params("parallel"),
        name="sample_q",
    )(q_all, q_all, *tabs, w_ukT)


PAGES_PER_STEP = 32


def _attn_sample_kernel(pt_ref, ql_ref, qp_ref, ln_ref, kn_ref, lat_hbm, kr_hbm, o_ref,
                        latbuf_ref, krbuf_ref, m_ref, l_ref, acc_ref, sem, *, G, R, layer, scale):
    b = pl.program_id(0)
    j = pl.program_id(1)
    nj = pl.num_programs(1)
    step = b * nj + j
    slot = step % 2
    PS = latbuf_ref.shape[2]

    def page_copies(bb, jj, sl, p):
        page = pt_ref[bb, jj * G + p]
        return (pltpu.make_async_copy(lat_hbm.at[layer, page], latbuf_ref.at[sl, p], sem.at[sl]),
                pltpu.make_async_copy(kr_hbm.at[layer, page], krbuf_ref.at[sl, p], sem.at[sl]))

    def fetch(bb, jj, sl):
        for p in range(G):
            for c in page_copies(bb, jj, sl, p):
                c.start()

    @pl.when(step == 0)
    def _():
        fetch(0, 0, 0)

    @pl.when(step + 1 < pl.num_programs(0) * nj)
    def _():
        last = j == nj - 1
        fetch(jnp.where(last, b + 1, b), jnp.where(last, 0, j + 1), 1 - slot)

    for p in range(G):
        for c in page_copies(b, j, slot, p):
            c.wait()

    ql = ql_ref[0]
    qp = qp_ref[0][:, :R].astype(BF16)

    @pl.when(j == 0)
    def _():
        lat_new = ln_ref[0].astype(BF16).astype(F32)
        kpe_new = kn_ref[0][:, :R].astype(BF16).astype(F32)
        s0 = (jnp.sum(ql.astype(F32) * lat_new, axis=-1, keepdims=True)
              + jnp.sum(qp.astype(F32) * kpe_new, axis=-1, keepdims=True)) * scale
        m_ref[...] = s0
        l_ref[...] = jnp.ones_like(s0)
        acc_ref[...] = jnp.broadcast_to(lat_new, acc_ref.shape)

    nt = (((1,), (1,)), ((), ()))
    lats = [latbuf_ref[slot, p].astype(BF16) for p in range(G)]
    ss = []
    for p in range(G):
        kr = krbuf_ref[slot, p].astype(BF16)
        ss.append(lax.dot_general(ql, lats[p], nt, preferred_element_type=F32)
                  + jnp.dot(qp, kr, preferred_element_type=F32))
    s = jnp.concatenate(ss, axis=1) * scale
    m = m_ref[...]
    m_new = jnp.maximum(m, jnp.max(s, axis=-1, keepdims=True))
    pr = jnp.exp(s - m_new)
    alpha = jnp.exp(m - m_new)
    l_ref[...] = alpha * l_ref[...] + jnp.sum(pr, axis=-1, keepdims=True)
    pb = pr.astype(BF16)
    pv = jnp.dot(pb[:, 0:PS], lats[0], preferred_element_type=F32)
    for p in range(1, G):
        pv = pv + jnp.dot(pb[:, p * PS:(p + 1) * PS], lats[p], preferred_element_type=F32)
    acc_ref[...] = alpha * acc_ref[...] + pv
    m_ref[...] = m_new

    @pl.when(j == pl.num_programs(1) - 1)
    def _():
        o_ref[0] = acc_ref[...] / l_ref[...]


def _attn_sample(page_table, qlat, qrope, lat_new, kpe_new, cache_latent, cache_kropeT, layer, scale):
    DB, H, KV = qlat.shape
    n_pages = page_table.shape[1]
    PS = cache_latent.shape[2]
    R = cache_kropeT.shape[2]
    G = _tile(n_pages, PAGES_PER_STEP, 1)

    per_b = lambda shape: pl.BlockSpec((1,) + shape, lambda b, j, pt: (b, 0, 0))
    grid_spec = pltpu.PrefetchScalarGridSpec(
        num_scalar_prefetch=1,
        grid=(DB, n_pages // G),
        in_specs=[per_b((H, KV)), per_b((H, LANE)), per_b((1, KV)), per_b((1, LANE)),
                  pl.BlockSpec(memory_space=pl.ANY), pl.BlockSpec(memory_space=pl.ANY)],
        out_specs=per_b((H, KV)),
        scratch_shapes=[pltpu.VMEM((2, G, PS, KV), F32), pltpu.VMEM((2, G, R, PS), F32),
                        pltpu.VMEM((H, 1), F32), pltpu.VMEM((H, 1), F32), pltpu.VMEM((H, KV), F32),
                        pltpu.SemaphoreType.DMA((2,))],
    )
    return pl.pallas_call(
        functools.partial(_attn_sample_kernel, G=G, R=R, layer=layer, scale=scale),
        grid_spec=grid_spec,
        out_shape=jax.ShapeDtypeStruct((DB, H, KV), F32),
        compiler_params=_params("arbitrary", "arbitrary"),
        name="attn_sample",
    )(page_table, qlat, qrope, lat_new, kpe_new, cache_latent, cache_kropeT)


def _sample_out_kernel(o_ref, w_ref, y_ref):
    y_ref[...] = jnp.dot(o_ref[...].astype(BF16), w_ref[0], preferred_element_type=F32).astype(y_ref.dtype)


def _sample_out(o_lat2d, w_uv3):
    DB = o_lat2d.shape[0]
    H, KV, DV = w_uv3.shape
    return pl.pallas_call(
        _sample_out_kernel,
        grid=(H,),
        in_specs=[pl.BlockSpec((DB, KV), lambda h: (0, h)), pl.BlockSpec((1, KV, DV), lambda h: (h, 0, 0))],
        out_specs=pl.BlockSpec((DB, DV), lambda h: (0, h)),
        out_shape=jax.ShapeDtypeStruct((DB, H * DV), BF16),
        compiler_params=_params("parallel"),
        name="sample_out",
    )(o_lat2d, w_uv3)


ACC_COLS = 512


def _accumulate_dot(o_ref, a, w_ref, first):
    N = o_ref.shape[1]
    cw = _tile(N, ACC_COLS, LANE)

    @pl.when(first)
    def _():
        o_ref[...] = jnp.zeros(o_ref.shape, o_ref.dtype)

    for c in range(N // cw):
        sl = slice(c * cw, (c + 1) * cw)
        o_ref[:, sl] += jnp.dot(a, w_ref[:, sl], preferred_element_type=F32)


def _proj_out_kernel(a_ref, w_ref, g_ref, x_ref, o_ref, *, tn):
    j = pl.program_id(1)
    cols = pl.ds(pl.multiple_of(j * tn, tn), tn)
    o_ref[:, cols] = jnp.dot(a_ref[...], w_ref[0, 0], preferred_element_type=F32)

    @pl.when(j == pl.num_programs(1) - 1)
    def _():
        _add_rms_inplace(o_ref, x_ref, g_ref)


def _proj_out(a, w_tiles, layer, g, x):
    M, K = a.shape
    _, nt, _, tn = w_tiles.shape
    N = nt * tn
    tm = _tile(M, 640, 16)
    return pl.pallas_call(
        functools.partial(_proj_out_kernel, tn=tn),
        grid=(M // tm, N // tn),
        in_specs=[
            pl.BlockSpec((tm, K), lambda i, j: (i, 0)),
            pl.BlockSpec((1, 1, K, tn), lambda i, j: (layer, j, 0, 0)),
            pl.BlockSpec((1, N), lambda i, j: (0, 0)),
            pl.BlockSpec((tm, N), lambda i, j: (i, 0), pipeline_mode=pl.Buffered(1)),
        ],
        out_specs=pl.BlockSpec((tm, N), lambda i, j: (i, 0)),
        out_shape=jax.ShapeDtypeStruct((M, N), F32),
        compiler_params=_params("parallel", "arbitrary"),
        name="proj_out",
    )(a, w_tiles, g, x)


def _swiglu_hidden(xn, wg, wu):
    g = jnp.dot(xn, wg, preferred_element_type=F32)
    u = jnp.dot(xn, wu, preferred_element_type=F32)
    return (g * jax.nn.sigmoid(g) * u).astype(BF16)


def _ffn_kernel(x_ref, gpre_ref, wg_ref, wu_ref, wd_ref, gpost_ref, o_ref, xn_ref):
    f = pl.program_id(1)

    @pl.when(f == 0)
    def _():
        _rms_to_bf16(x_ref, gpre_ref, xn_ref)

    h = _swiglu_hidden(xn_ref[...], wg_ref[0, 0], wu_ref[0, 0])
    _accumulate_dot(o_ref, h, wd_ref, f == 0)

    @pl.when(f == pl.num_programs(1) - 1)
    def _():
        _add_rms_inplace(o_ref, x_ref, gpost_ref)


def _ffn(x, g_pre, wg_tiles, wu_tiles, layer, wd, g_post):
    M, D = x.shape
    _, nf, _, tf = wg_tiles.shape
    F = nf * tf
    tm = _tile(M, 640, 16)
    return pl.pallas_call(
        _ffn_kernel,
        grid=(M // tm, F // tf),
        in_specs=[
            pl.BlockSpec((tm, D), lambda i, f: (i, 0), pipeline_mode=pl.Buffered(1)),
            pl.BlockSpec((1, D), lambda i, f: (0, 0)),
            pl.BlockSpec((1, 1, D, tf), lambda i, f: (layer, f, 0, 0)),
            pl.BlockSpec((1, 1, D, tf), lambda i, f: (layer, f, 0, 0)),
            pl.BlockSpec((tf, D), lambda i, f: (f, 0)),
            pl.BlockSpec((1, D), lambda i, f: (0, 0)),
        ],
        out_specs=pl.BlockSpec((tm, D), lambda i, f: (i, 0)),
        out_shape=jax.ShapeDtypeStruct((M, D), F32),
        scratch_shapes=[pltpu.VMEM((tm, D), BF16)],
        compiler_params=_params("parallel", "arbitrary"),
        name="ffn",
    )(x, g_pre, wg_tiles, wu_tiles, wd, g_post)


def _router_kernel(x_ref, g_ref, w_ref, ti_ref, tg_ref, *, E):
    xn = _rms(x_ref[...], g_ref[...])
    logits = jnp.dot(xn, w_ref[...], preferred_element_type=F32, precision=lax.Precision.HIGHEST)
    lane = lax.broadcasted_iota(jnp.int32, logits.shape, 1)
    logits = jnp.where(lane < E, logits, -jnp.inf)
    mx = jnp.max(logits, axis=-1, keepdims=True)
    ex = jnp.exp(logits - mx)
    probs = ex / jnp.sum(ex, axis=-1, keepdims=True)
    p1 = jnp.max(probs, axis=-1, keepdims=True)
    i1 = jnp.min(jnp.where(probs == p1, lane, LANE), axis=-1, keepdims=True)
    rest = jnp.where(lane == i1, -1.0, probs)
    p2 = jnp.max(rest, axis=-1, keepdims=True)
    i2 = jnp.min(jnp.where(rest == p2, lane, LANE), axis=-1, keepdims=True)
    den = p1 + p2
    ti_ref[...] = jnp.where(lane == 0, i1, jnp.where(lane == 1, i2, 0))
    tg_ref[...] = jnp.where(lane == 0, p1 / den, jnp.where(lane == 1, p2 / den, 0.0))


def _router(x, g, w_router_pad, E):
    M, D = x.shape
    tm = _tile(M, 320, 16)
    return pl.pallas_call(
        functools.partial(_router_kernel, E=E),
        grid=(M // tm,),
        in_specs=[
            pl.BlockSpec((tm, D), lambda i: (i, 0)),
            pl.BlockSpec((1, D), lambda i: (0, 0)),
            pl.BlockSpec((D, LANE), lambda i: (0, 0)),
        ],
        out_specs=[pl.BlockSpec((tm, LANE), lambda i: (i, 0)), pl.BlockSpec((tm, LANE), lambda i: (i, 0))],
        out_shape=[jax.ShapeDtypeStruct((M, LANE), jnp.int32), jax.ShapeDtypeStruct((M, LANE), F32)],
        compiler_params=_params("parallel"),
        name="router",
    )(x, g, w_router_pad)


def _row_gather(src_hbm, dst_ref, sem, n, index_of):
    def issue(r, c):
        pltpu.make_async_copy(src_hbm.at[pl.ds(index_of(r), 1)], dst_ref.at[pl.ds(r, 1)], sem).start()
        return c

    lax.fori_loop(0, n, issue, 0)

    def drain(r, c):
        pltpu.make_async_copy(src_hbm.at[pl.ds(0, 1)], dst_ref.at[pl.ds(r, 1)], sem).wait()
        return c

    lax.fori_loop(0, n, drain, 0)


def _experts_kernel(te_ref, nt_ref, tok_ref, x_hbm, gpre_ref, wg_ref, wu_ref, wd_ref, o_ref,
                    xbuf_ref, xn_ref, sem, *, tm, nf):
    i = pl.program_id(0)
    f = pl.program_id(1)
    nt = nt_ref[0]
    rps = tm // nf

    def row_copy(tile, r):
        return pltpu.make_async_copy(x_hbm.at[pl.ds(tok_ref[tile * tm + r], 1)], xbuf_ref.at[pl.ds(r, 1)], sem)

    def for_rows(fn):
        def body(r, c):
            fn(r)
            return c
        lax.fori_loop(0, tm, body, 0)

    @pl.when(i < nt)
    def _():
        @pl.when(f == 0)
        def _():
            @pl.when(i == 0)
            def _():
                for_rows(lambda r: row_copy(0, r).start())

            for_rows(lambda r: row_copy(i, r).wait())
            _rms_to_bf16(xbuf_ref, gpre_ref, xn_ref)

        @pl.when(i + 1 < nt)
        def _():
            for r in range(rps):
                row_copy(i + 1, f * rps + r).start()

        h = _swiglu_hidden(xn_ref[...], wg_ref[0, 0], wu_ref[0, 0])
        _accumulate_dot(o_ref, h, wd_ref.at[0], f == 0)

    @pl.when((i >= nt_ref[0]) & (f == 0))
    def _():
        o_ref[...] = jnp.zeros(o_ref.shape, o_ref.dtype)


def _experts(tile_expert, n_tiles, row_token, x, g_pre, wg_tiles, wu_tiles, e0, wd, tm, n_tiles_max):
    D = x.shape[1]
    _, nf, _, tf = wg_tiles.shape
    assert tm % nf == 0

    def live(i, nt):
        return jnp.minimum(i, nt[0] - 1)

    def fidx(i, f, nt):
        return jnp.where(i < nt[0], f, nf - 1)

    grid_spec = pltpu.PrefetchScalarGridSpec(
        num_scalar_prefetch=3,
        grid=(n_tiles_max, nf),
        in_specs=[
            pl.BlockSpec(memory_space=pl.ANY),
            pl.BlockSpec((1, D), lambda i, f, te, nt, tok: (0, 0)),
            pl.BlockSpec((1, 1, D, tf), lambda i, f, te, nt, tok: (e0 + te[live(i, nt)], fidx(i, f, nt), 0, 0)),
            pl.BlockSpec((1, 1, D, tf), lambda i, f, te, nt, tok: (e0 + te[live(i, nt)], fidx(i, f, nt), 0, 0)),
            pl.BlockSpec((1, tf, D), lambda i, f, te, nt, tok: (te[live(i, nt)], fidx(i, f, nt), 0)),
        ],
        out_specs=pl.BlockSpec((tm, D), lambda i, f, te, nt, tok: (i, 0)),
        scratch_shapes=[pltpu.VMEM((tm, D), F32), pltpu.VMEM((tm, D), BF16), pltpu.SemaphoreType.DMA(())],
    )
    return pl.pallas_call(
        functools.partial(_experts_kernel, tm=tm, nf=nf),
        grid_spec=grid_spec,
        out_shape=jax.ShapeDtypeStruct((n_tiles_max * tm, D), F32),
        compiler_params=_params("arbitrary", "arbitrary"),
        name="experts",
    )(tile_expert, n_tiles, row_token, x, g_pre, wg_tiles, wu_tiles, wd)


def _combine_kernel(pos_ref, y_hbm, x_ref, tg_ref, g_ref, o_ref, ybuf_ref, sem, *, tc):
    i = pl.program_id(0)
    _row_gather(y_hbm, ybuf_ref, sem, 2 * tc, lambda r: pos_ref[i * 2 * tc + r])
    tg = tg_ref[...]
    mixed = tg[:, 0:1] * ybuf_ref[0:tc, :] + tg[:, 1:2] * ybuf_ref[tc:2 * tc, :]
    o_ref[...] = x_ref[...] + _rms(mixed, g_ref[...])


def _combine(pos_tiles, y, x, top_g, g_post, tc):
    M, D = x.shape
    grid_spec = pltpu.PrefetchScalarGridSpec(
        num_scalar_prefetch=1,
        grid=(M // tc,),
        in_specs=[
            pl.BlockSpec(memory_space=pl.ANY),
            pl.BlockSpec((tc, D), lambda i, pos: (i, 0)),
            pl.BlockSpec((tc, LANE), lambda i, pos: (i, 0)),
            pl.BlockSpec((1, D), lambda i, pos: (0, 0)),
        ],
        out_specs=pl.BlockSpec((tc, D), lambda i, pos: (i, 0)),
        scratch_shapes=[pltpu.VMEM((2 * tc, D), F32), pltpu.SemaphoreType.DMA(())],
    )
    return pl.pallas_call(
        functools.partial(_combine_kernel, tc=tc),
        grid_spec=grid_spec,
        out_shape=jax.ShapeDtypeStruct((M, D), F32),
        compiler_params=_params("arbitrary"),
        name="combine",
    )(pos_tiles, y, x, top_g, g_post)


def _moe(x, g_pre, w_router, wg_tiles, wu_tiles, e0, wd, g_post):
    M, D = x.shape
    E = w_router.shape[1]
    wr = jnp.zeros((D, LANE), F32).at[:, :E].set(w_router)
    top_i, top_g = _router(x, g_pre, wr, E)
    tm = _tile(M, 512, 16) if M < 512 else 512
    e_flat = top_i[:, :2].reshape(-1)
    onehot = (e_flat[:, None] == jnp.arange(E, dtype=jnp.int32)[None, :]).astype(jnp.int32)
    counts = jnp.sum(onehot, axis=0)
    rank = jnp.sum((jnp.cumsum(onehot, axis=0) - onehot) * onehot, axis=1)
    padded = ((counts + tm - 1) // tm) * tm
    ends = jnp.cumsum(padded)
    pos = (ends - padded)[e_flat] + rank
    n_tiles_max = -(-2 * M // tm) + E
    row_token = jnp.zeros((n_tiles_max * tm,), jnp.int32).at[pos].set(jnp.arange(2 * M, dtype=jnp.int32) // 2)
    n_tiles = (ends[-1] // tm).astype(jnp.int32).reshape(1)
    tile_start = jnp.arange(n_tiles_max, dtype=jnp.int32) * tm
    tile_expert = jnp.minimum(jnp.searchsorted(ends, tile_start, side="right"), E - 1).astype(jnp.int32)
    y = _experts(tile_expert, n_tiles, row_token, x, g_pre, wg_tiles, wu_tiles, e0, wd, tm, n_tiles_max)
    tc = _tile(M, 128, SUBLANE)
    pos_tiles = pos.reshape(M // tc, tc, 2).transpose(0, 2, 1).reshape(-1).astype(jnp.int32)
    return _combine(pos_tiles, y, x, top_g, g_post, tc)


def _ple_kernel(x_ref, xc_ref, g_ref, wg_ref, p_ref, wp_ref, o_ref, xn_ref):
    @pl.when(pl.program_id(1) == 0)
    def _():
        _rms_to_bf16(x_ref, g_ref, xn_ref)

    gate = jax.nn.sigmoid(jnp.dot(xn_ref[...], wg_ref[0, 0], preferred_element_type=F32))
    emb = jnp.dot(p_ref[...].astype(BF16), wp_ref[...], preferred_element_type=F32)
    o_ref[...] = xc_ref[...] + gate * emb


def _ple(x, g, wg_tiles, layer, p, w_ple):
    M, D = x.shape
    P = p.shape[1]
    _, nt, _, tn = wg_tiles.shape
    tm = _tile(M, 640, 16)
    return pl.pallas_call(
        _ple_kernel,
        grid=(M // tm, D // tn),
        in_specs=[
            pl.BlockSpec((tm, D), lambda i, j: (i, 0)),
            pl.BlockSpec((tm, tn), lambda i, j: (i, j)),
            pl.BlockSpec((1, D), lambda i, j: (0, 0)),
            pl.BlockSpec((1, 1, D, tn), lambda i, j: (layer, j, 0, 0)),
            pl.BlockSpec((tm, P), lambda i, j: (i, 0)),
            pl.BlockSpec((P, tn), lambda i, j: (0, j)),
        ],
        out_specs=pl.BlockSpec((tm, tn), lambda i, j: (i, j)),
        out_shape=jax.ShapeDtypeStruct((M, D), F32),
        scratch_shapes=[pltpu.VMEM((tm, D), BF16)],
        compiler_params=_params("parallel", "arbitrary"),
        name="ple",
    )(x, x, g, wg_tiles, p, w_ple)


def _rope_tables(pos):
    half = 32
    inv = ROPE_BASE ** (-jnp.arange(0, 2 * half, 2, dtype=F32) / (2 * half))
    ang = pos.astype(F32)[:, None] * inv[None, :]
    cos, sin = jnp.cos(ang), jnp.sin(ang)
    z = jnp.zeros_like(cos)
    return (jnp.concatenate([cos, cos, z, z], axis=1),
            jnp.concatenate([-sin, z, z, z], axis=1),
            jnp.concatenate([z, sin, z, z], axis=1))


def kernel(x_prompt, x_sample, p_prompt, p_sample, state_conv_a, state_conv_b, state_pool, cache_latent, cache_krope, page_table, g_pre_mix, w_in, conv_a_w, conv_a_b, ln_a_g, ln_a_b, conv_b_w, pool_w, pool_scale, g_q, w_uq, g_kv, w_uk, w_uv, w_out, g_post_mix, g_pre_ffn, g_post_ffn, w_ffn_gate, w_ffn_up, w_ffn_down, w_router, w_exp_gate, w_exp_up, w_exp_down, g_ple, w_ple_gate, w_ple):
    B, S, D = x_prompt.shape
    DB, T, _ = x_sample.shape
    assert T == 1
    depth = w_in.shape[0]
    Np = B * S
    M = Np + DB * T
    W_A = conv_a_w.shape[2]
    W_B = conv_b_w.shape[2]
    W_C = pool_scale.shape[1]
    assert W_A == W_B == W_C
    Q_RANK = g_q.shape[1]
    KV = g_kv.shape[1]
    H = w_uq.shape[2]
    NOPE = w_uk.shape[3]
    ROPE = cache_krope.shape[3]
    DV = w_uv.shape[3]
    assert NOPE == LANE and DV == LANE and ROPE == 64 and w_uq.shape[3] == NOPE + ROPE
    NB = state_pool.shape[2]
    n_past = page_table.shape[1] * cache_latent.shape[2]
    scale = float(NOPE + ROPE) ** -0.5
    N_main = 2 * W_A + 3 * W_B + W_C + Q_RANK + KV
    assert w_in.shape[2] == N_main + ROPE
    col_a, col_b, col_c = 0, 2 * W_A, 2 * W_A + 3 * W_B
    col_q = col_c + W_C
    col_kv = col_q + Q_RANK
    assert col_q % Q_RANK == 0 and col_kv % KV == 0

    x = jnp.concatenate([x_prompt.reshape(Np, D), x_sample.reshape(DB, D)], axis=0)
    pos = jnp.concatenate([jnp.tile(jnp.arange(S), B), jnp.full((DB,), n_past)])
    tabs = _rope_tables(pos)
    cache_kropeT = jnp.transpose(cache_krope, (0, 1, 3, 2))
    row2 = lambda a: a.reshape(1, -1)
    outs = [[] for _ in range(10)]

    E = w_router.shape[2]
    D_FF, D_FFE = w_ffn_gate.shape[2], w_exp_gate.shape[3]
    w_in_t = _col_tiles(w_in, _tile(N_main, 768, LANE), n_cols=N_main)
    w_out_t = _col_tiles(w_out, _tile(D, 512, LANE))
    w_pg_t = _col_tiles(w_ple_gate, _tile(D, 512, LANE))
    tf_d, tf_e = _tile(D_FF, 256, LANE), _tile(D_FFE, 256, LANE)
    w_fg_t, w_fu_t = _col_tiles(w_ffn_gate, tf_d), _col_tiles(w_ffn_up, tf_d)
    w_eg_t = _col_tiles(w_exp_gate.reshape(-1, D, D_FFE), tf_e)
    w_eu_t = _col_tiles(w_exp_up.reshape(-1, D, D_FFE), tf_e)

    for l in range(depth):
        w_tail =jnp.pad(w_in[l, :, N_main:], ((0, 0), (0, LANE - ROPE))).astype(BF16)
        wq = w_uq[l]
        wq_r = jnp.concatenate(
            [wq[:, :, :NOPE].reshape(Q_RANK, H * NOPE),
             jnp.pad(wq[:, :, NOPE:], ((0, 0), (0, 0), (0, LANE - ROPE))).reshape(Q_RANK, H * LANE)],
            axis=1).astype(BF16)
        w_kv = jnp.concatenate([w_uk[l].reshape(KV, H * NOPE), w_uv[l].reshape(KV, H * DV)], axis=1).astype(BF16)
        w_ukT = jnp.transpose(w_uk[l], (1, 2, 0)).astype(BF16)
        w_uv3 = jnp.transpose(w_uv[l], (1, 0, 2)).astype(BF16)

        u, kpe_raw = _proj_in(x, row2(g_pre_mix[l]), w_in_t, l, w_tail)
        latent, kpe = _latent(u, col_kv // KV, KV, row2(g_kv[l]), kpe_raw, tabs)
        q_all = _matmul(u, wq_r, row2(g_q[l]), x_col=col_q // Q_RANK)

        ya_pre, na_p = _mixa_prompt(u, B, S, W_A, conv_a_w[l], row2(conv_a_b[l]))
        ya_p = _ln_silu(ya_pre, row2(ln_a_g[l]), row2(ln_a_b[l]))
        yb_p, nb_p = _mixb_prompt(u, col_b, B, S, W_B, conv_b_w[l])
        pw = pool_w[l].astype(BF16)
        yc_p, nc_p = _mixc_prompt(u, col_c, B, S, W_C, pw, row2(pool_scale[l]), NB)
        kv = _matmul(latent, w_kv, rows=Np, out_dtype=BF16)
        yd_p = _attn_prompt(q_all, tabs, kv, kpe.astype(BF16), B, S, H, scale)

        sa = jnp.transpose(state_conv_a[l], (1, 0, 2))
        sb = jnp.transpose(state_conv_b[l], (1, 0, 2))
        sc = jnp.transpose(state_pool[l], (1, 0, 2))
        cols = (col_a, col_a + W_A, col_b, col_b + W_B, col_b + 2 * W_B, col_c)
        ya_s, yb_s, yc_s, na_s, nb_s, nc_s = _mix_sample(
            u, Np, cols, DB, W_A, sa, sb, sc, conv_a_w[l], row2(conv_a_b[l]), row2(ln_a_g[l]),
            row2(ln_a_b[l]), conv_b_w[l], pw, row2(pool_scale[l]), n_past)
        qlat, qrope = _sample_q(q_all, tabs, Np, DB, H, w_ukT)
        o_lat = _attn_sample(page_table, qlat.reshape(DB, H, KV), qrope.reshape(DB, H, LANE),
                             latent[Np:].reshape(DB, 1, KV), kpe[Np:].reshape(DB, 1, LANE),
                             cache_latent, cache_kropeT, l, scale)
        yd_s = _sample_out(o_lat.reshape(DB, H * KV), w_uv3)

        mix = jnp.concatenate([jnp.concatenate([ya_p, yb_p, yc_p, yd_p], axis=1),
                               jnp.concatenate([ya_s, yb_s, yc_s, yd_s], axis=1)], axis=0)
        x = _proj_out(mix, w_out_t, l, row2(g_post_mix[l]), x)

        j = l // 2
        if l % 2 == 0:
            x = _ffn(x, row2(g_pre_ffn[l]), w_fg_t, w_fu_t, j,
                     w_ffn_down[j].astype(BF16), row2(g_post_ffn[l]))
        else:
            x = _moe(x, row2(g_pre_ffn[l]), w_router[j], w_eg_t, w_eu_t, j * E,
                     w_exp_down[j].astype(BF16), row2(g_post_ffn[l]))

        p_l = jnp.concatenate([p_prompt[l].reshape(Np, -1), p_sample[l].reshape(DB, -1)], axis=0)
        x = _ple(x, row2(g_ple[l]), w_pg_t, l, p_l, w_ple[l].astype(BF16))

        new = (na_p, jnp.transpose(na_s, (1, 0, 2)), nb_p, jnp.transpose(nb_s, (1, 0, 2)),
               nc_p, jnp.transpose(nc_s, (1, 0, 2)),
               latent[:Np].reshape(B, S, KV), latent[Np:].reshape(DB, T, KV),
               kpe[:Np, :ROPE].reshape(B, S, ROPE), kpe[Np:, :ROPE].reshape(DB, T, ROPE))
        for o, v in zip(outs, new):
            o.append(v)

    return (x[:Np].reshape(B, S, D), x[Np:].reshape(DB, T, D)) + tuple(jnp.stack(o) for o in outs)
```
